```python
import jax, jax.numpy as jnp
from jax import lax
import numpy as np

D_MODEL = 4096
BATCH = 4
SEQ = 4096
DEPTH = 1

GRID_W = 64
CTX_LEN = 256
HEAD_DIM = 128
POOL_W = D_MODEL // 4
ATTN_W = D_MODEL - POOL_W
N_HEADS = ATTN_W // HEAD_DIM
N_KV_HEADS = N_HEADS // 3
Q_PER_KV = N_HEADS // N_KV_HEADS
KV_W = N_KV_HEADS * HEAD_DIM
IN_W = ATTN_W + 2 * KV_W + POOL_W
ROPE_PAIRS_PER_AXIS = HEAD_DIM // 4
ROPE_THETA = 10000.0
Q_BLOCK = 128
POOL_WINDOWS = (2, 4, 8, 16)
POOL_GROUP_W = POOL_W // len(POOL_WINDOWS)
N_EXPERTS = 32
TOP_K = 4
EXPERT_FF = 1536
SWIGLU_ALPHA = 1.702
SWIGLU_LIMIT = 7.0
MOE_BLOCK = 256
N_MOD = 6
EPS = 1e-6
ATTN_SCALE = HEAD_DIM ** -0.5

kernel_name = 'hybrid_attn_pool_moe_flow_layer'


def rmsnorm(x, g):
    xf = x.astype(jnp.float32)
    y = xf * lax.rsqrt(jnp.mean(xf * xf, axis=-1, keepdims=True) + EPS)
    return (y * g.astype(jnp.float32)).astype(x.dtype)


def modulate(h, shift, scale):
    return h * (1 + scale) + shift


def axial_rope_tables(seq_len):
    rows = seq_len // GRID_W
    row = jnp.repeat(jnp.arange(rows, dtype=jnp.float32), GRID_W)
    col = jnp.tile(jnp.arange(GRID_W, dtype=jnp.float32), rows)
    freqs = ROPE_THETA ** (-jnp.arange(ROPE_PAIRS_PER_AXIS, dtype=jnp.float32) / ROPE_PAIRS_PER_AXIS)
    ang = jnp.concatenate([row[:, None] * freqs, col[:, None] * freqs], axis=-1)
    return jnp.cos(ang), jnp.sin(ang)


def apply_rope(x, cos, sin):
    half = HEAD_DIM // 2
    cos = cos[None, :, None, :].astype(x.dtype)
    sin = sin[None, :, None, :].astype(x.dtype)
    x1, x2 = x[..., :half], x[..., half:]
    return jnp.concatenate([x1 * cos - x2 * sin, x1 * sin + x2 * cos], axis=-1)


def split_in(z):
    return (z[..., :ATTN_W], z[..., ATTN_W:ATTN_W + KV_W],
            z[..., ATTN_W + KV_W:ATTN_W + 2 * KV_W], z[..., ATTN_W + 2 * KV_W:])


def attention(q, k, v):
    s = jnp.einsum('bqkgd,bskd->bkgqs', q, k, preferred_element_type=jnp.float32) * ATTN_SCALE
    p = jax.nn.softmax(s, axis=-1).astype(v.dtype)
    o = jnp.einsum('bkgqs,bskd->bqkgd', p, v)
    return o.reshape(o.shape[0], o.shape[1], -1)


def blocked_attention(q, k, v):
    B, L = q.shape[:2]
    qb = jnp.moveaxis(q.reshape(B, L // Q_BLOCK, Q_BLOCK, *q.shape[2:]), 1, 0)
    o = lax.map(lambda qi: attention(qi, k, v), qb)
    return jnp.moveaxis(o, 0, 1).reshape(B, L, -1)


def pool_mixer(u, w_pool, pool_scale):
    B, L, _ = u.shape
    uf = u.astype(jnp.float32).reshape(B, L, len(POOL_WINDOWS), POOL_GROUP_W)
    cs = jnp.pad(jnp.cumsum(uf, axis=1), ((0, 0), (1, 0), (0, 0), (0, 0)))
    t = jnp.arange(L)
    outs = []
    for gi, w in enumerate(POOL_WINDOWS):
        lo = jnp.clip(t - w // 2, 0, L)
        hi = jnp.clip(t + w // 2, 0, L)
        cg = cs[:, :, gi]
        cnt = (hi - lo).astype(jnp.float32)[None, :, None]
        outs.append((cg[:, hi] - cg[:, lo]) / cnt - uf[:, :, gi])
    d = jnp.stack(outs, axis=2).astype(u.dtype)
    y = jnp.einsum('blgc,gce->blge', d, w_pool)
    return y.reshape(B, L, POOL_W) * pool_scale


def moe_ffn(h, w_router, b_router, w_gate, b_gate, w_up, b_up, w_down, b_down):
    n, d = h.shape
    logits = jnp.einsum('nd,de->ne', h, w_router, preferred_element_type=jnp.float32) + b_router.astype(jnp.float32)
    top_logit, top_idx = lax.top_k(logits, TOP_K)
    gate = jax.nn.softmax(top_logit, axis=-1)
    n_assign = n * TOP_K
    e_flat = top_idx.reshape(n_assign)
    order = jnp.argsort(e_flat)
    e_sorted = e_flat[order]
    tok_sorted = (order // TOP_K).astype(jnp.int32)
    gate_sorted = gate.reshape(n_assign)[order]
    counts = jnp.bincount(e_flat, length=N_EXPERTS)
    starts = jnp.cumsum(counts) - counts
    padded = (counts + MOE_BLOCK - 1) // MOE_BLOCK * MOE_BLOCK
    padded_end = jnp.cumsum(padded)
    dest = padded_end[e_sorted] - padded[e_sorted] + jnp.arange(n_assign) - starts[e_sorted]
    n_blocks = -(-(n_assign + N_EXPERTS * (MOE_BLOCK - 1)) // MOE_BLOCK)
    slots = n_blocks * MOE_BLOCK
    tok_buf = jnp.zeros((slots,), jnp.int32).at[dest].set(tok_sorted).reshape(n_blocks, MOE_BLOCK)
    gate_buf = jnp.zeros((slots,), jnp.float32).at[dest].set(gate_sorted).reshape(n_blocks, MOE_BLOCK)
    block_expert = jnp.minimum(
        jnp.searchsorted(padded_end, jnp.arange(n_blocks) * MOE_BLOCK, side='right'), N_EXPERTS - 1)

    def expert_block(acc, blk):
        tok, g, e = blk
        xb = h[tok]
        a = jnp.minimum(xb @ w_gate[e] + b_gate[e], SWIGLU_LIMIT)
        b = jnp.clip(xb @ w_up[e] + b_up[e], -SWIGLU_LIMIT, SWIGLU_LIMIT)
        y = ((b + 1) * (a * jax.nn.sigmoid(SWIGLU_ALPHA * a))) @ w_down[e] + b_down[e]
        return acc.at[tok].add(y.astype(jnp.float32) * g[:, None]), None

    acc, _ = lax.scan(expert_block, jnp.zeros((n, d), jnp.float32), (tok_buf, gate_buf, block_expert))
    return acc.astype(h.dtype)


def setup_inputs(seed: int = 0) -> dict:
    key = jax.random.key(seed)
    ks = jax.random.split(key, 23)
    f32 = jnp.float32

    def nrm(k, shape, scale):
        return jax.random.normal(k, shape, f32) * scale

    L = DEPTH
    return {
        'x': nrm(ks[0], (BATCH, SEQ, D_MODEL), 1.0),
        'c': nrm(ks[1], (BATCH, D_MODEL), 1.0),
        'ctx': nrm(ks[2], (BATCH, CTX_LEN, D_MODEL), 1.0),
        'c_ctx': nrm(ks[3], (D_MODEL,), 1.0),
        'w_ada': nrm(ks[4], (L, D_MODEL, N_MOD * D_MODEL), 0.5 * D_MODEL ** -0.5),
        'b_ada': nrm(ks[5], (L, N_MOD * D_MODEL), 0.02),
        'norm1_g': 1.0 + nrm(ks[6], (L, D_MODEL), 0.05),
        'w_in': nrm(ks[7], (L, D_MODEL, IN_W), D_MODEL ** -0.5),
        'q_norm_g': 1.0 + nrm(ks[8], (L, HEAD_DIM), 0.05),
        'k_norm_g': 1.0 + nrm(ks[9], (L, HEAD_DIM), 0.05),
        'w_pool': nrm(ks[10], (L, len(POOL_WINDOWS), POOL_GROUP_W, POOL_GROUP_W), POOL_GROUP_W ** -0.5),
        'pool_scale': 1.0 + nrm(ks[11], (L, POOL_W), 0.1),
        'w_out': nrm(ks[12], (L, D_MODEL, D_MODEL), D_MODEL ** -0.5),
        'norm2_g': 1.0 + nrm(ks[13], (L, D_MODEL), 0.05),
        'w_router': nrm(ks[14], (L, D_MODEL, N_EXPERTS), D_MODEL ** -0.5),
        'b_router': nrm(ks[15], (L, N_EXPERTS), 0.01),
        'w_gate': nrm(ks[16], (L, N_EXPERTS, D_MODEL, EXPERT_FF), D_MODEL ** -0.5),
        'b_gate': nrm(ks[17], (L, N_EXPERTS, EXPERT_FF), 0.02),
        'w_up': nrm(ks[18], (L, N_EXPERTS, D_MODEL, EXPERT_FF), D_MODEL ** -0.5),
        'b_up': nrm(ks[19], (L, N_EXPERTS, EXPERT_FF), 0.02),
        'w_down': nrm(ks[20], (L, N_EXPERTS, EXPERT_FF, D_MODEL), EXPERT_FF ** -0.5),
        'b_down': nrm(ks[21], (L, N_EXPERTS, D_MODEL), 0.02),
        'final_g': 1.0 + nrm(ks[22], (D_MODEL,), 0.05),
    }


def reference(x, c, ctx, c_ctx, w_ada, b_ada, norm1_g, w_in, q_norm_g, k_norm_g, w_pool, pool_scale,
              w_out, norm2_g, w_router, b_router, w_gate, b_gate, w_up, b_up, w_down, b_down, final_g):
    B, S, D = x.shape
    C = ctx.shape[1]
    cos, sin = axial_rope_tables(S)
    for l in range(DEPTH):
        last = l == DEPTH - 1
        mod = (jnp.einsum('bd,de->be', jax.nn.silu(c), w_ada[l]) + b_ada[l]).reshape(B, N_MOD, 1, D)
        mod_c = (jax.nn.silu(c_ctx) @ w_ada[l] + b_ada[l]).reshape(N_MOD, 1, D)

        h = modulate(rmsnorm(x, norm1_g[l]), mod[:, 0], mod[:, 1])
        hc = modulate(rmsnorm(ctx, norm1_g[l]), mod_c[0], mod_c[1])
        q, k, v, u = split_in(jnp.einsum('bld,de->ble', h, w_in[l]))
        q = apply_rope(rmsnorm(q.reshape(B, S, N_HEADS, HEAD_DIM), q_norm_g[l]), cos, sin)
        k = apply_rope(rmsnorm(k.reshape(B, S, N_KV_HEADS, HEAD_DIM), k_norm_g[l]), cos, sin)
        v = v.reshape(B, S, N_KV_HEADS, HEAD_DIM)
        if last:
            kc, vc = jnp.split(jnp.einsum('bld,de->ble', hc, w_in[l][:, ATTN_W:ATTN_W + 2 * KV_W]), 2, axis=-1)
        else:
            qc, kc, vc, uc = split_in(jnp.einsum('bld,de->ble', hc, w_in[l]))
        kc = rmsnorm(kc.reshape(B, C, N_KV_HEADS, HEAD_DIM), k_norm_g[l])
        vc = vc.reshape(B, C, N_KV_HEADS, HEAD_DIM)
        k_all = jnp.concatenate([kc, k], axis=1)
        v_all = jnp.concatenate([vc, v], axis=1)
        attn = blocked_attention(q.reshape(B, S, N_KV_HEADS, Q_PER_KV, HEAD_DIM), k_all, v_all)
        mix = jnp.concatenate([attn, pool_mixer(u, w_pool[l], pool_scale[l])], axis=-1)
        x = x + mod[:, 2] * jnp.einsum('ble,ed->bld', mix, w_out[l])
        if not last:
            qc = rmsnorm(qc.reshape(B, C, N_HEADS, HEAD_DIM), q_norm_g[l]).reshape(B, C, N_KV_HEADS, Q_PER_KV, HEAD_DIM)
            mix_c = jnp.concatenate([attention(qc, kc, vc), pool_mixer(uc, w_pool[l], pool_scale[l])], axis=-1)
            ctx = ctx + mod_c[2] * jnp.einsum('ble,ed->bld', mix_c, w_out[l])

        h2 = modulate(rmsnorm(x, norm2_g[l]), mod[:, 3], mod[:, 4]).reshape(B * S, D)
        moe_w = (w_router[l], b_router[l], w_gate[l], b_gate[l], w_up[l], b_up[l], w_down[l], b_down[l])
        if last:
            x = x + mod[:, 5] * moe_ffn(h2, *moe_w).reshape(B, S, D)
        else:
            h2c = modulate(rmsnorm(ctx, norm2_g[l]), mod_c[3], mod_c[4]).reshape(B * C, D)
            f = moe_ffn(jnp.concatenate([h2, h2c], axis=0), *moe_w)
            x = x + mod[:, 5] * f[:B * S].reshape(B, S, D)
            ctx = ctx + mod_c[5] * f[B * S:].reshape(B, C, D)
    return rmsnorm(x, final_g)
```

```python
import functools

import jax
import jax.numpy as jnp
from jax import lax
from jax.experimental import pallas as pl
from jax.experimental.pallas import tpu as pltpu

F32 = jnp.float32
BF16 = jnp.bfloat16
I32 = jnp.int32

EPS = 1e-6
HEAD_DIM = 128
Q_PER_KV = 3
GRID_W = 64
ROPE_THETA = 10000.0
POOL_WINDOWS = (2, 4, 8, 16)
TOP_K = 4
SWIGLU_ALPHA = 1.702
SWIGLU_LIMIT = 7.0
N_MOD = 6
ATTN_SCALE = HEAD_DIM ** -0.5

V7X_VMEM_BYTES = 64 * 1024 * 1024
VMEM_LIMIT = V7X_VMEM_BYTES * 7 // 8
LANES = 128
NORM_ROW_CHUNK = 256
MOD_ROWS = 16
HI_MASK = -65536


def _params(*sem):
    return pltpu.CompilerParams(dimension_semantics=sem, vmem_limit_bytes=VMEM_LIMIT)


def _pack_bf16_pair(lo, hi):
    lo_bits = lax.bitcast_convert_type(lo.astype(BF16).astype(F32), I32)
    hi_bits = lax.bitcast_convert_type(hi.astype(BF16).astype(F32), I32)
    return lax.shift_right_logical(lo_bits, 16) | hi_bits


def _unpack_bf16_pair(p):
    lo = lax.bitcast_convert_type(lax.shift_left(p, 16), F32)
    hi = lax.bitcast_convert_type(p & HI_MASK, F32)
    return lo, hi


def _adaln_kernel(c_ref, w_ref, b_ref, o_ref):
    c = c_ref[...]
    a = (c * jax.nn.sigmoid(c)).astype(BF16)
    o_ref[...] = jnp.dot(a, w_ref[...].astype(BF16), preferred_element_type=F32) + b_ref[...]


def _adaln(cc, w_ada, b_ada):
    d = cc.shape[1]
    n = w_ada.shape[2]
    tn = min(512, n)
    return pl.pallas_call(
        _adaln_kernel,
        grid=(n // tn,),
        in_specs=[
            pl.BlockSpec((MOD_ROWS, d), lambda j: (0, 0)),
            pl.BlockSpec((None, d, tn), lambda j: (0, 0, j)),
            pl.BlockSpec((1, tn), lambda j: (0, j)),
        ],
        out_specs=pl.BlockSpec((MOD_ROWS, tn), lambda j: (0, j)),
        out_shape=jax.ShapeDtypeStruct((MOD_ROWS, n), F32),
        compiler_params=_params("arbitrary"),
    )(cc, w_ada, b_ada)


def _inproj_kernel(x_ref, mod_ref, g_ref, w_ref, qg_ref, kg_ref, cos_ref, sin_ref, o_ref, h_scr,
                   *, nq, nk, rope, heads_per_tile):
    j = pl.program_id(2)

    @pl.when(j == 0)
    def _():
        rc = min(NORM_ROW_CHUNK, x_ref.shape[0])

        def chunk(r, carry):
            rows = pl.ds(pl.multiple_of(r * rc, rc), rc)
            xf = x_ref[rows, :]
            ms = jnp.mean(xf * xf, axis=-1, keepdims=True)
            y = xf * lax.rsqrt(ms + EPS) * g_ref[...]
            h_scr[rows, :] = (y * (1 + mod_ref[1:2, :]) + mod_ref[0:1, :]).astype(BF16)
            return carry

        lax.fori_loop(0, x_ref.shape[0] // rc, chunk, 0)

    z = jnp.dot(h_scr[...], w_ref[...], preferred_element_type=F32)

    def norm_rope(g):
        for hh in range(heads_per_tile):
            zc = z[:, hh * HEAD_DIM:(hh + 1) * HEAD_DIM]
            ms = jnp.mean(zc * zc, axis=-1, keepdims=True)
            zn = zc * lax.rsqrt(ms + EPS) * g
            if rope:
                zn = zn * cos_ref[...] + pltpu.roll(zn, HEAD_DIM // 2, 1) * sin_ref[...]
            o_ref[:, hh * HEAD_DIM:(hh + 1) * HEAD_DIM] = zn.astype(BF16)

    if nq:
        @pl.when(j < nq)
        def _():
            norm_rope(qg_ref[...])

    @pl.when((j >= nq) & (j < nq + nk))
    def _():
        norm_rope(kg_ref[...])

    @pl.when(j >= nq + nk)
    def _():
        o_ref[...] = z.astype(BF16)


def _inproj(x, mod3, mod_row, norm_g, w_in, col0, ncols, nq_cols, nk_cols, q_g, k_g, cosf, sinf, rope):
    b, l, d = x.shape
    tm = min(1024, l)
    tn = min(512, ncols)
    kern = functools.partial(_inproj_kernel, nq=nq_cols // tn, nk=nk_cols // tn, rope=rope,
                             heads_per_tile=tn // HEAD_DIM)
    if mod_row is None:
        mod_map = lambda bi, i, j: (bi, 0, 0)
    else:
        mod_map = lambda bi, i, j: (mod_row, 0, 0)
    return pl.pallas_call(
        kern,
        grid=(b, l // tm, ncols // tn),
        in_specs=[
            pl.BlockSpec((None, tm, d), lambda bi, i, j: (bi, i, 0), pipeline_mode=pl.Buffered(1)),
            pl.BlockSpec((None, N_MOD, d), mod_map),
            pl.BlockSpec((1, d), lambda bi, i, j: (0, 0)),
            pl.BlockSpec((None, d, tn), lambda bi, i, j: (0, 0, j + col0 // tn)),
            pl.BlockSpec((1, HEAD_DIM), lambda bi, i, j: (0, 0)),
            pl.BlockSpec((1, HEAD_DIM), lambda bi, i, j: (0, 0)),
            pl.BlockSpec((tm, HEAD_DIM), lambda bi, i, j: (i, 0)),
            pl.BlockSpec((tm, HEAD_DIM), lambda bi, i, j: (i, 0)),
        ],
        out_specs=pl.BlockSpec((None, tm, tn), lambda bi, i, j: (bi, i, j)),
        out_shape=jax.ShapeDtypeStruct((b, l, ncols), BF16),
        scratch_shapes=[pltpu.VMEM((tm, d), BF16)],
        compiler_params=_params("arbitrary", "arbitrary", "arbitrary"),
    )(x, mod3, norm_g, w_in, q_g, k_g, cosf, sinf)


def _attn_kernel(q_ref, kc_ref, vc_ref, k_ref, v_ref, o_ref, *, tq, tk, n_chunks):
    q = q_ref[...]
    qs = jnp.concatenate([q[:, g * HEAD_DIM:(g + 1) * HEAD_DIM] for g in range(Q_PER_KV)], axis=0)
    rows = Q_PER_KV * tq

    def step(k, v, carry):
        m, l, acc = carry
        s = lax.dot_general(qs, k, (((1,), (1,)), ((), ())), preferred_element_type=F32) * ATTN_SCALE
        m_new = jnp.maximum(m, jnp.max(s, axis=-1, keepdims=True))
        alpha = jnp.exp(m - m_new)
        p = jnp.exp(s - m_new)
        l = alpha * l + jnp.sum(p, axis=-1, keepdims=True)
        acc = alpha * acc + jnp.dot(p.astype(BF16), v, preferred_element_type=F32)
        return m_new, l, acc

    carry = (jnp.full((rows, 1), -jnp.inf, F32), jnp.zeros((rows, 1), F32), jnp.zeros((rows, HEAD_DIM), F32))
    carry = step(kc_ref[...], vc_ref[...], carry)

    def body(i, c):
        off = pl.multiple_of(i * tk, tk)
        return step(k_ref[pl.ds(off, tk), :], v_ref[pl.ds(off, tk), :], c)

    m, l, acc = lax.fori_loop(0, n_chunks, body, carry)
    o = acc / l
    for g in range(Q_PER_KV):
        o_ref[:, g * HEAD_DIM:(g + 1) * HEAD_DIM] = o[g * tq:(g + 1) * tq, :].astype(BF16)


def _attention(z, zc, n_kv):
    b, s, _ = z.shape
    c = zc.shape[1]
    tq = min(256, s)
    tk = min(512, s)
    gw = Q_PER_KV * HEAD_DIM
    n_q = n_kv * Q_PER_KV
    kern = functools.partial(_attn_kernel, tq=tq, tk=tk, n_chunks=s // tk)
    return pl.pallas_call(
        kern,
        grid=(b, n_kv, s // tq),
        in_specs=[
            pl.BlockSpec((None, tq, gw), lambda bi, g, i: (bi, i, g)),
            pl.BlockSpec((None, c, HEAD_DIM), lambda bi, g, i: (bi, 0, g)),
            pl.BlockSpec((None, c, HEAD_DIM), lambda bi, g, i: (bi, 0, n_kv + g)),
            pl.BlockSpec((None, s, HEAD_DIM), lambda bi, g, i: (bi, 0, n_q + g)),
            pl.BlockSpec((None, s, HEAD_DIM), lambda bi, g, i: (bi, 0, n_q + n_kv + g)),
        ],
        out_specs=pl.BlockSpec((None, tq, gw), lambda bi, g, i: (bi, i, g)),
        out_shape=jax.ShapeDtypeStruct((b, s, n_q * HEAD_DIM), BF16),
        compiler_params=_params("arbitrary", "arbitrary", "arbitrary"),
    )(z, zc, zc, z, z)


def _pool_kernel(u_ref, w_ref, sc_ref, o_ref, d_scr):
    g = pl.program_id(1)
    n, gw = u_ref.shape
    t = lax.broadcasted_iota(I32, (n, LANES), 0)

    def shifted(a, d):
        r = pltpu.roll(a, (-d) % n, 0)
        ok = (t + d < n) if d > 0 else (t + d >= 0)
        return jnp.where(ok, r, 0.0)

    for gi, w in enumerate(POOL_WINDOWS):
        @pl.when(g == gi)
        def _(w=w):
            half = w // 2
            cnt = (jnp.minimum(t + half, n) - jnp.maximum(t - half, 0)).astype(F32)
            for c0 in range(0, gw, LANES):
                uf = u_ref[:, c0:c0 + LANES].astype(F32)
                fwd = uf
                bwd = shifted(uf, -1)
                span = 1
                while span < half:
                    fwd = fwd + shifted(fwd, span)
                    bwd = bwd + shifted(bwd, -span)
                    span *= 2
                d_scr[:, c0:c0 + LANES] = ((fwd + bwd) / cnt - uf).astype(BF16)

    y = jnp.dot(d_scr[...], w_ref[...].astype(BF16), preferred_element_type=F32)
    o_ref[...] = (y * sc_ref[...]).astype(BF16)


def _pool(z, w_pool, pool_scale, col0):
    b, s, _ = z.shape
    ng, gw = w_pool.shape[1], w_pool.shape[2]
    return pl.pallas_call(
        _pool_kernel,
        grid=(b, ng),
        in_specs=[
            pl.BlockSpec((None, s, gw), lambda bi, g: (bi, 0, col0 // gw + g)),
            pl.BlockSpec((None, None, gw, gw), lambda bi, g: (0, g, 0, 0)),
            pl.BlockSpec((None, 1, gw), lambda bi, g: (g, 0, 0)),
        ],
        out_specs=pl.BlockSpec((None, s, gw), lambda bi, g: (bi, 0, g)),
        out_shape=jax.ShapeDtypeStruct((b, s, ng * gw), BF16),
        scratch_shapes=[pltpu.VMEM((s, gw), BF16)],
        compiler_params=_params("arbitrary", "arbitrary"),
    )(z, w_pool, pool_scale.reshape(ng, 1, gw))


def _outproj_kernel(a_ref, p_ref, wa_ref, wp_ref, x_ref, mod_ref, o_ref):
    y = jnp.dot(a_ref[...], wa_ref[...], preferred_element_type=F32)
    y = y + jnp.dot(p_ref[...], wp_ref[...], preferred_element_type=F32)
    o_ref[...] = x_ref[...] + mod_ref[2:3, :] * y


def _outproj(attn, pool, w_out, x, mod3):
    b, s, d = x.shape
    aw, pw = attn.shape[2], pool.shape[2]
    tm = min(1024, s)
    tn = min(512, d)
    return pl.pallas_call(
        _outproj_kernel,
        grid=(b, s // tm, d // tn),
        in_specs=[
            pl.BlockSpec((None, tm, aw), lambda bi, i, j: (bi, i, 0)),
            pl.BlockSpec((None, tm, pw), lambda bi, i, j: (bi, i, 0)),
            pl.BlockSpec((None, aw, tn), lambda bi, i, j: (0, 0, j)),
            pl.BlockSpec((None, pw, tn), lambda bi, i, j: (0, aw // pw, j)),
            pl.BlockSpec((None, tm, tn), lambda bi, i, j: (bi, i, j)),
            pl.BlockSpec((None, N_MOD, tn), lambda bi, i, j: (bi, 0, j)),
        ],
        out_specs=pl.BlockSpec((None, tm, tn), lambda bi, i, j: (bi, i, j)),
        out_shape=jax.ShapeDtypeStruct((b, s, d), F32),
        compiler_params=_params("arbitrary", "arbitrary", "arbitrary"),
    )(attn, pool, w_out, w_out, x, mod3)


def _router_kernel(x_ref, mod_ref, g_ref, wr_ref, br_ref, hp_ref, mi_ref, mg_ref, cnt_ref, carry_scr,
                   *, tm, ne):
    first = (pl.program_id(0) == 0) & (pl.program_id(1) == 0)

    @pl.when(first)
    def _():
        carry_scr[...] = jnp.zeros_like(carry_scr)

    xf = x_ref[...]
    ms = jnp.mean(xf * xf, axis=-1, keepdims=True)
    y = xf * lax.rsqrt(ms + EPS) * g_ref[...]
    h = y * (1 + mod_ref[4:5, :]) + mod_ref[3:4, :]
    half = h.shape[1] // 2
    hp_ref[...] = _pack_bf16_pair(h[:, :half], h[:, half:])
    logits = jnp.dot(h.astype(BF16), wr_ref[...].astype(BF16), preferred_element_type=F32) + br_ref[...]

    lane = lax.broadcasted_iota(I32, logits.shape, 1).astype(F32)
    work = logits
    chosen = jnp.zeros(logits.shape, jnp.bool_)
    hits, idxs, vals = [], [], []
    for _ in range(TOP_K):
        m = jnp.max(work, axis=-1, keepdims=True)
        ik = jnp.min(jnp.where(work == m, lane, float(ne)), axis=-1, keepdims=True)
        hit = lane == ik
        hits.append(hit)
        idxs.append(ik.astype(I32))
        vals.append(m)
        chosen = chosen | hit
        work = jnp.where(hit, -jnp.inf, work)

    exps = [jnp.exp(v - vals[0]) for v in vals]
    den = exps[0]
    for e in exps[1:]:
        den = den + e
    gates = [e / den for e in exps]

    sel = chosen.astype(F32)
    r = lax.broadcasted_iota(I32, (tm, tm), 0)
    c = lax.broadcasted_iota(I32, (tm, tm), 1)
    tri = (c < r).astype(BF16)
    rank = carry_scr[...] + jnp.dot(tri, sel.astype(BF16), preferred_element_type=F32)
    carry_scr[...] = carry_scr[...] + jnp.sum(sel, axis=0, keepdims=True)
    cnt_ref[...] = carry_scr[...]
    ranks = [jnp.sum(jnp.where(hit, rank, 0.0), axis=-1, keepdims=True).astype(I32) for hit in hits]

    lane128 = lax.broadcasted_iota(I32, (tm, LANES), 1)
    mi = jnp.zeros((tm, LANES), I32)
    mg = jnp.zeros((tm, LANES), F32)
    for k in range(TOP_K):
        mi = jnp.where(lane128 == k, idxs[k], mi)
        mi = jnp.where(lane128 == TOP_K + k, ranks[k], mi)
        mg = jnp.where(lane128 == k, gates[k], mg)
    mi_ref[...] = mi
    mg_ref[...] = mg


def _router(x1, mod3, norm_g, w_router, b_router):
    b, s, d = x1.shape
    ne = w_router.shape[2]
    tm = min(256, s)
    nt = s // tm
    n = b * s
    kern = functools.partial(_router_kernel, tm=tm, ne=ne)
    return pl.pallas_call(
        kern,
        grid=(b, nt),
        in_specs=[
            pl.BlockSpec((None, tm, d), lambda bi, i: (bi, i, 0)),
            pl.BlockSpec((None, N_MOD, d), lambda bi, i: (bi, 0, 0)),
            pl.BlockSpec((1, d), lambda bi, i: (0, 0)),
            pl.BlockSpec((None, d, ne), lambda bi, i: (0, 0, 0)),
            pl.BlockSpec((1, ne), lambda bi, i: (0, 0)),
        ],
        out_specs=[
            pl.BlockSpec((tm, d // 2), lambda bi, i: (bi * nt + i, 0)),
            pl.BlockSpec((tm, LANES), lambda bi, i: (bi * nt + i, 0)),
            pl.BlockSpec((tm, LANES), lambda bi, i: (bi * nt + i, 0)),
            pl.BlockSpec((1, ne), lambda bi, i: (0, 0)),
        ],
        out_shape=[
            jax.ShapeDtypeStruct((n, d // 2), I32),
            jax.ShapeDtypeStruct((n, LANES), I32),
            jax.ShapeDtypeStruct((n, LANES), F32),
            jax.ShapeDtypeStruct((1, ne), F32),
        ],
        scratch_shapes=[pltpu.VMEM((1, ne), F32)],
        compiler_params=_params("arbitrary", "arbitrary"),
    )(x1, mod3, norm_g, w_router, b_router)


DISPATCH_WINDOW = 16


def _dispatch_kernel(pos_ref, hp_ref, xs_in_ref, xs_ref, sem, *, td):
    del xs_in_ref
    base = pl.program_id(0) * (td * TOP_K)

    def copy(t, k):
        p = pos_ref[base + t * TOP_K + k]
        return pltpu.make_async_copy(hp_ref.at[pl.ds(t, 1)], xs_ref.at[pl.ds(p, 1)], sem)

    def body(t, carry):
        for k in range(TOP_K):
            copy(t, k).start()

        @pl.when(t >= DISPATCH_WINDOW)
        def _():
            for k in range(TOP_K):
                copy(t - DISPATCH_WINDOW, k).wait()

        return carry

    lax.fori_loop(0, td, body, 0)

    def drain(t, carry):
        for k in range(TOP_K):
            copy(t, k).wait()
        return carry

    lax.fori_loop(td - DISPATCH_WINDOW, td, drain, 0)


def _dispatch(pos_flat, hp, rows):
    n, w = hp.shape
    td = min(512, n)
    kern = functools.partial(_dispatch_kernel, td=td)
    zeros = jnp.zeros((rows, w), I32)
    return pl.pallas_call(
        kern,
        grid_spec=pltpu.PrefetchScalarGridSpec(
            num_scalar_prefetch=1,
            grid=(n // td,),
            in_specs=[
                pl.BlockSpec((td, w), lambda i, pos: (i, 0)),
                pl.BlockSpec(memory_space=pl.ANY),
            ],
            out_specs=pl.BlockSpec(memory_space=pl.ANY),
            scratch_shapes=[pltpu.SemaphoreType.DMA],
        ),
        out_shape=jax.ShapeDtypeStruct((rows, w), I32),
        input_output_aliases={2: 0},
        compiler_params=_params("arbitrary"),
    )(pos_flat, hp, zeros)


def _moe_up_kernel(te_ref, nu_ref, xs_ref, wg_ref, bg_ref, wu_ref, bu_ref, hid_ref, xb_scr):
    i = pl.program_id(0)
    j = pl.program_id(1)

    @pl.when(i < nu_ref[0])
    def _():
        @pl.when(j == 0)
        def _():
            lo, hi = _unpack_bf16_pair(xs_ref[...])
            half = lo.shape[1]
            xb_scr[:, :half] = lo.astype(BF16)
            xb_scr[:, half:] = hi.astype(BF16)

        xb = xb_scr[...]
        a = jnp.dot(xb, wg_ref[...].astype(BF16), preferred_element_type=F32) + bg_ref[...]
        u = jnp.dot(xb, wu_ref[...].astype(BF16), preferred_element_type=F32) + bu_ref[...]
        a = jnp.minimum(a, SWIGLU_LIMIT)
        u = jnp.clip(u, -SWIGLU_LIMIT, SWIGLU_LIMIT)
        hid_ref[...] = ((u + 1) * (a * jax.nn.sigmoid(SWIGLU_ALPHA * a))).astype(BF16)

    @pl.when(i >= nu_ref[0])
    def _():
        hid_ref[...] = jnp.zeros_like(hid_ref)


def _moe_up(tile_expert, n_used, xs, w_gate, b_gate, w_up, b_up, tm):
    rows, half = xs.shape
    d = 2 * half
    ne, ff = w_gate.shape[1], w_gate.shape[3]
    tf = min(256, ff)
    nf = ff // tf

    def row(i, nu):
        return jnp.minimum(i, nu[0] - 1)

    def col(i, j, nu):
        return jnp.where(i < nu[0], j, nf - 1)

    w_spec = pl.BlockSpec((None, None, d, tf), lambda i, j, te, nu: (0, te[row(i, nu)], 0, col(i, j, nu)))
    b_spec = pl.BlockSpec((None, 1, tf), lambda i, j, te, nu: (te[row(i, nu)], 0, col(i, j, nu)))
    return pl.pallas_call(
        _moe_up_kernel,
        grid_spec=pltpu.PrefetchScalarGridSpec(
            num_scalar_prefetch=2,
            grid=(rows // tm, nf),
            in_specs=[
                pl.BlockSpec((tm, half), lambda i, j, te, nu: (row(i, nu), 0)),
                w_spec, b_spec, w_spec, b_spec,
            ],
            out_specs=pl.BlockSpec((tm, tf), lambda i, j, te, nu: (i, j)),
            scratch_shapes=[pltpu.VMEM((tm, d), BF16)],
        ),
        out_shape=jax.ShapeDtypeStruct((rows, ff), BF16),
        compiler_params=_params("arbitrary", "arbitrary"),
    )(tile_expert, n_used, xs, w_gate, b_gate.reshape(ne, 1, ff), w_up, b_up.reshape(ne, 1, ff))


def _moe_down_kernel(te_ref, nu_ref, hid_ref, wlo_ref, whi_ref, blo_ref, bhi_ref, ys_ref):
    @pl.when(pl.program_id(0) < nu_ref[0])
    def _():
        h = hid_ref[...]
        ylo = jnp.dot(h, wlo_ref[...].astype(BF16), preferred_element_type=F32) + blo_ref[...]
        yhi = jnp.dot(h, whi_ref[...].astype(BF16), preferred_element_type=F32) + bhi_ref[...]
        ys_ref[...] = _pack_bf16_pair(ylo, yhi)

    @pl.when(pl.program_id(0) >= nu_ref[0])
    def _():
        ys_ref[...] = jnp.zeros_like(ys_ref)


def _moe_down(tile_expert, n_used, hid, w_down, b_down, tm):
    rows, ff = hid.shape
    ne, d = w_down.shape[1], w_down.shape[3]
    half = d // 2
    tn = min(512, half)
    nj = half // tn

    def row(i, nu):
        return jnp.minimum(i, nu[0] - 1)

    def col(i, j, nu):
        return jnp.where(i < nu[0], j, nj - 1)

    def w_spec(off):
        return pl.BlockSpec((None, None, ff, tn),
                            lambda i, j, te, nu: (0, te[row(i, nu)], 0, col(i, j, nu) + off))

    def b_spec(off):
        return pl.BlockSpec((None, 1, tn), lambda i, j, te, nu: (te[row(i, nu)], 0, col(i, j, nu) + off))

    return pl.pallas_call(
        _moe_down_kernel,
        grid_spec=pltpu.PrefetchScalarGridSpec(
            num_scalar_prefetch=2,
            grid=(rows // tm, nj),
            in_specs=[
                pl.BlockSpec((tm, ff), lambda i, j, te, nu: (row(i, nu), 0)),
                w_spec(0), w_spec(nj), b_spec(0), b_spec(nj),
            ],
            out_specs=pl.BlockSpec((tm, tn), lambda i, j, te, nu: (i, j)),
        ),
        out_shape=jax.ShapeDtypeStruct((rows, half), I32),
        compiler_params=_params("arbitrary", "arbitrary"),
    )(tile_expert, n_used, hid, w_down, w_down, b_down.reshape(ne, 1, d), b_down.reshape(ne, 1, d))


def _combine_kernel(pos_ref, ys_ref, x_ref, mg_ref, mod_ref, fg_ref, o_ref, ybuf, sem, *, tc, nt):
    base = (pl.program_id(0) * nt + pl.program_id(1)) * (tc * TOP_K)

    def copy(t, k):
        p = pos_ref[base + t * TOP_K + k]
        return pltpu.make_async_copy(ys_ref.at[pl.ds(p, 1)], ybuf.at[k, pl.ds(t, 1)], sem)

    def issue(t, carry):
        for k in range(TOP_K):
            copy(t, k).start()
        return carry

    def wait(t, carry):
        for k in range(TOP_K):
            copy(t, k).wait()
        return carry

    lax.fori_loop(0, tc, issue, 0)
    lax.fori_loop(0, tc, wait, 0)

    half = ybuf.shape[2]
    acc_lo = jnp.zeros((tc, half), F32)
    acc_hi = jnp.zeros((tc, half), F32)
    for k in range(TOP_K):
        lo, hi = _unpack_bf16_pair(ybuf[k])
        g = mg_ref[:, k:k + 1]
        acc_lo = acc_lo + lo * g
        acc_hi = acc_hi + hi * g
    x_lo = x_ref[:, :half] + mod_ref[5:6, :half] * acc_lo
    x_hi = x_ref[:, half:] + mod_ref[5:6, half:] * acc_hi
    ms = (jnp.sum(x_lo * x_lo, axis=-1, keepdims=True) + jnp.sum(x_hi * x_hi, axis=-1, keepdims=True)) / (2 * half)
    inv = lax.rsqrt(ms + EPS)
    o_ref[:, :half] = x_lo * inv * fg_ref[:, :half]
    o_ref[:, half:] = x_hi * inv * fg_ref[:, half:]


def _combine(pos_flat, ys, x1, mg, mod3, final_g):
    b, s, d = x1.shape
    tc = min(256, s)
    nt = s // tc
    kern = functools.partial(_combine_kernel, tc=tc, nt=nt)
    return pl.pallas_call(
        kern,
        grid_spec=pltpu.PrefetchScalarGridSpec(
            num_scalar_prefetch=1,
            grid=(b, nt),
            in_specs=[
                pl.BlockSpec(memory_space=pl.ANY),
                pl.BlockSpec((None, tc, d), lambda bi, i, pos: (bi, i, 0)),
                pl.BlockSpec((tc, LANES), lambda bi, i, pos: (bi * nt + i, 0)),
                pl.BlockSpec((None, N_MOD, d), lambda bi, i, pos: (bi, 0, 0)),
                pl.BlockSpec((1, d), lambda bi, i, pos: (0, 0)),
            ],
            out_specs=pl.BlockSpec((None, tc, d), lambda bi, i, pos: (bi, i, 0)),
            scratch_shapes=[pltpu.VMEM((TOP_K, tc, d // 2), I32), pltpu.SemaphoreType.DMA],
        ),
        out_shape=jax.ShapeDtypeStruct((b, s, d), F32),
        compiler_params=_params("arbitrary", "arbitrary"),
    )(pos_flat, ys, x1, mg, mod3, final_g)


def _rope_tables(seq_len):
    pairs = HEAD_DIM // 4
    rows = seq_len // GRID_W
    row = jnp.repeat(jnp.arange(rows, dtype=F32), GRID_W)
    col = jnp.tile(jnp.arange(GRID_W, dtype=F32), rows)
    freqs = ROPE_THETA ** (-jnp.arange(pairs, dtype=F32) / pairs)
    ang = jnp.concatenate([row[:, None] * freqs, col[:, None] * freqs], axis=-1)
    cos, sin = jnp.cos(ang), jnp.sin(ang)
    return jnp.concatenate([cos, cos], axis=-1), jnp.concatenate([-sin, sin], axis=-1)


def _moe_tile_rows(n_tokens):
    return min(512, n_tokens)


def kernel(x, c, ctx, c_ctx, w_ada, b_ada, norm1_g, w_in, q_norm_g, k_norm_g, w_pool, pool_scale, w_out,
           norm2_g, w_router, b_router, w_gate, b_gate, w_up, b_up, w_down, b_down, final_g):
    b, s, d = x.shape
    assert w_ada.shape[0] == 1, "single-layer trunk"
    assert b + 1 <= MOD_ROWS
    pool_w = d // 4
    attn_w = d - pool_w
    n_kv = attn_w // HEAD_DIM // Q_PER_KV
    kv_w = n_kv * HEAD_DIM
    ne = w_router.shape[2]

    cc = jnp.concatenate([c, c_ctx[None, :], jnp.zeros((MOD_ROWS - b - 1, d), F32)], axis=0)
    mod3 = _adaln(cc, w_ada, b_ada).reshape(MOD_ROWS, N_MOD, d)

    cosf, sinf = _rope_tables(s)
    w_in = w_in.astype(BF16)
    w_out = w_out.astype(BF16)
    z = _inproj(x, mod3, None, norm1_g, w_in, 0, attn_w + 2 * kv_w + pool_w, attn_w, kv_w,
                q_norm_g, k_norm_g, cosf, sinf, True)
    zc = _inproj(ctx, mod3, b, norm1_g, w_in, attn_w, 2 * kv_w, 0, kv_w,
                 q_norm_g, k_norm_g, cosf, sinf, False)
    attn = _attention(z, zc, n_kv)
    pool = _pool(z, w_pool, pool_scale, attn_w + 2 * kv_w)
    x1 = _outproj(attn, pool, w_out, x, mod3)

    hp, mi, mg, counts = _router(x1, mod3, norm2_g, w_router, b_router)
    n = b * s
    tm = _moe_tile_rows(n)
    counts = counts[0].astype(I32)
    padded = (counts + tm - 1) // tm * tm
    pend = jnp.cumsum(padded)
    pstart = pend - padded
    idx4 = mi[:, :TOP_K]
    pos_flat = (pstart[idx4] + mi[:, TOP_K:2 * TOP_K]).reshape(n * TOP_K)
    n_tiles = n * TOP_K // tm + ne
    n_used = (pend[-1:] // tm).astype(I32)
    tile_expert = jnp.minimum(
        jnp.searchsorted(pend, jnp.arange(n_tiles, dtype=I32) * tm, side='right'), ne - 1).astype(I32)

    xs = _dispatch(pos_flat, hp, n_tiles * tm)
    hid = _moe_up(tile_expert, n_used, xs, w_gate, b_gate[0], w_up, b_up[0], tm)
    ys = _moe_down(tile_expert, n_used, hid, w_down, b_down[0], tm)
    return _combine(pos_flat, ys, x1, mg, mod3, final_g[None, :])
```

```python
import functools

import jax
import jax.numpy as jnp
from jax import lax
from jax.experimental import pallas as pl
from jax.experimental.pallas import tpu as pltpu

F32 = jnp.float32
BF16 = jnp.bfloat16
I32 = jnp.int32

EPS = 1e-6
HEAD_DIM = 128
Q_PER_KV = 3
GRID_W = 64
ROPE_THETA = 10000.0
POOL_WINDOWS = (2, 4, 8, 16)
TOP_K = 4
SWIGLU_ALPHA = 1.702
SWIGLU_LIMIT = 7.0
N_MOD = 6
ATTN_SCALE = HEAD_DIM ** -0.5
LOG2_E = 1.4426950408889634

V7X_VMEM_BYTES = 64 * 1024 * 1024
VMEM_LIMIT = V7X_VMEM_BYTES * 7 // 8
LANES = 128
SUBLANES = 8
ATTN_ROW_BLOCK = 128
NORM_ROW_CHUNK = 256
MOD_ROWS = 16
HI_MASK = -65536


def _params(*sem):
    return pltpu.CompilerParams(dimension_semantics=sem, vmem_limit_bytes=VMEM_LIMIT)


def _pack_bf16_pair(lo, hi):
    lo_bits = lax.bitcast_convert_type(lo.astype(BF16).astype(F32), I32)
    hi_bits = lax.bitcast_convert_type(hi.astype(BF16).astype(F32), I32)
    return lax.shift_right_logical(lo_bits, 16) | hi_bits


def _unpack_bf16_pair(p):
    lo = lax.bitcast_convert_type(lax.shift_left(p, 16), F32)
    hi = lax.bitcast_convert_type(p & HI_MASK, F32)
    return lo, hi


def _adaln_kernel(c_ref, w_ref, b_ref, o_ref):
    c = c_ref[...]
    a = (c * jax.nn.sigmoid(c)).astype(BF16)
    o_ref[...] = jnp.dot(a, w_ref[...].astype(BF16), preferred_element_type=F32) + b_ref[...]


def _adaln(cc, w_ada, b_ada):
    d = cc.shape[1]
    n = w_ada.shape[2]
    tn = min(512, n)
    return pl.pallas_call(
        _adaln_kernel,
        grid=(n // tn,),
        in_specs=[
            pl.BlockSpec((MOD_ROWS, d), lambda j: (0, 0)),
            pl.BlockSpec((None, d, tn), lambda j: (0, 0, j)),
            pl.BlockSpec((1, tn), lambda j: (0, j)),
        ],
        out_specs=pl.BlockSpec((MOD_ROWS, tn), lambda j: (0, j)),
        out_shape=jax.ShapeDtypeStruct((MOD_ROWS, n), F32),
        compiler_params=_params("arbitrary"),
    )(cc, w_ada, b_ada)


def _inproj_kernel(x_ref, mod_ref, g_ref, w_ref, qg_ref, kg_ref, cos_ref, sin_ref, o_ref, h_scr,
                   *, nq, nk, rope, heads_per_tile):
    j = pl.program_id(2)

    @pl.when(j == 0)
    def _():
        rc = min(NORM_ROW_CHUNK, x_ref.shape[0])

        def chunk(r, carry):
            rows = pl.ds(pl.multiple_of(r * rc, rc), rc)
            xf = x_ref[rows, :]
            ms = jnp.mean(xf * xf, axis=-1, keepdims=True)
            y = xf * lax.rsqrt(ms + EPS) * g_ref[...]
            h_scr[rows, :] = (y * (1 + mod_ref[1:2, :]) + mod_ref[0:1, :]).astype(BF16)
            return carry

        lax.fori_loop(0, x_ref.shape[0] // rc, chunk, 0)

    z = jnp.dot(h_scr[...], w_ref[...], preferred_element_type=F32)

    def norm_rope(g):
        for hh in range(heads_per_tile):
            zc = z[:, hh * HEAD_DIM:(hh + 1) * HEAD_DIM]
            ms = jnp.mean(zc * zc, axis=-1, keepdims=True)
            zn = zc * lax.rsqrt(ms + EPS) * g
            if rope:
                zn = zn * cos_ref[...] + pltpu.roll(zn, HEAD_DIM // 2, 1) * sin_ref[...]
            o_ref[:, hh * HEAD_DIM:(hh + 1) * HEAD_DIM] = zn.astype(BF16)

    if nq:
        @pl.when(j < nq)
        def _():
            norm_rope(qg_ref[...])

    @pl.when((j >= nq) & (j < nq + nk))
    def _():
        norm_rope(kg_ref[...])

    @pl.when(j >= nq + nk)
    def _():
        o_ref[...] = z.astype(BF16)


def _inproj(x, mod3, mod_row, norm_g, w_in, col0, ncols, nq_cols, nk_cols, q_g, k_g, cosf, sinf, rope):
    b, l, d = x.shape
    tm = min(1024, l)
    tn = min(512, ncols)
    kern = functools.partial(_inproj_kernel, nq=nq_cols // tn, nk=nk_cols // tn, rope=rope,
                             heads_per_tile=tn // HEAD_DIM)
    if mod_row is None:
        mod_map = lambda bi, i, j: (bi, 0, 0)
    else:
        mod_map = lambda bi, i, j: (mod_row, 0, 0)
    return pl.pallas_call(
        kern,
        grid=(b, l // tm, ncols // tn),
        in_specs=[
            pl.BlockSpec((None, tm, d), lambda bi, i, j: (bi, i, 0), pipeline_mode=pl.Buffered(1)),
            pl.BlockSpec((None, N_MOD, d), mod_map),
            pl.BlockSpec((1, d), lambda bi, i, j: (0, 0)),
            pl.BlockSpec((None, d, tn), lambda bi, i, j: (0, 0, j + col0 // tn)),
            pl.BlockSpec((1, HEAD_DIM), lambda bi, i, j: (0, 0)),
            pl.BlockSpec((1, HEAD_DIM), lambda bi, i, j: (0, 0)),
            pl.BlockSpec((tm, HEAD_DIM), lambda bi, i, j: (i, 0)),
            pl.BlockSpec((tm, HEAD_DIM), lambda bi, i, j: (i, 0)),
        ],
        out_specs=pl.BlockSpec((None, tm, tn), lambda bi, i, j: (bi, i, j)),
        out_shape=jax.ShapeDtypeStruct((b, l, ncols), BF16),
        scratch_shapes=[pltpu.VMEM((tm, d), BF16)],
        compiler_params=_params("arbitrary", "arbitrary", "arbitrary"),
    )(x, mod3, norm_g, w_in, q_g, k_g, cosf, sinf)


def _attn_kernel(q_ref, kc_ref, vc_ref, k_ref, v_ref, o_ref):
    nt = (((1,), (1,)), ((), ()))
    vc = jnp.concatenate([vc_ref[...], jnp.ones(vc_ref.shape, BF16)], axis=1)
    vl = jnp.concatenate([v_ref[...], jnp.ones(v_ref.shape, BF16)], axis=1)
    tq = q_ref.shape[0]
    rb = min(ATTN_ROW_BLOCK, tq)
    blocks = [(r0, g * HEAD_DIM) for g in range(Q_PER_KV) for r0 in range(0, tq, rb)]

    def scores(blk):
        r0, c0 = blk
        q = q_ref[r0:r0 + rb, c0:c0 + HEAD_DIM]
        return (lax.dot_general(q, kc_ref[...], nt, preferred_element_type=F32),
                lax.dot_general(q, k_ref[...], nt, preferred_element_type=F32))

    nxt = scores(blocks[0])
    for bi, (r0, c0) in enumerate(blocks):
        sc, sl = nxt
        if bi + 1 < len(blocks):
            nxt = scores(blocks[bi + 1])
        m = jnp.maximum(jnp.max(sc, axis=-1, keepdims=True), jnp.max(sl, axis=-1, keepdims=True))
        pc = jnp.exp2((sc - m).astype(BF16))
        pk = jnp.exp2((sl - m).astype(BF16))
        o = jnp.dot(pc, vc, preferred_element_type=F32) + jnp.dot(pk, vl, preferred_element_type=F32)
        o_ref[r0:r0 + rb, c0:c0 + HEAD_DIM] = (o[:, :HEAD_DIM] / o[:, HEAD_DIM:HEAD_DIM + 1]).astype(BF16)


def _attention(z, zc, n_kv):
    b, s, _ = z.shape
    c = zc.shape[1]
    tq = min(512, s)
    gw = Q_PER_KV * HEAD_DIM
    n_q = n_kv * Q_PER_KV
    return pl.pallas_call(
        _attn_kernel,
        grid=(b, n_kv, s // tq),
        in_specs=[
            pl.BlockSpec((None, tq, gw), lambda bi, g, i: (bi, i, g)),
            pl.BlockSpec((None, c, HEAD_DIM), lambda bi, g, i: (bi, 0, g)),
            pl.BlockSpec((None, c, HEAD_DIM), lambda bi, g, i: (bi, 0, n_kv + g)),
            pl.BlockSpec((None, s, HEAD_DIM), lambda bi, g, i: (bi, 0, n_q + g)),
            pl.BlockSpec((None, s, HEAD_DIM), lambda bi, g, i: (bi, 0, n_q + n_kv + g)),
        ],
        out_specs=pl.BlockSpec((None, tq, gw), lambda bi, g, i: (bi, i, g)),
        out_shape=jax.ShapeDtypeStruct((b, s, n_q * HEAD_DIM), BF16),
        compiler_params=_params("arbitrary", "arbitrary", "arbitrary"),
    )(z, zc, zc, z, z)


def _pool_kernel(u_ref, w_ref, sc_ref, o_ref, d_scr):
    g = pl.program_id(1)
    n, gw = u_ref.shape
    t = lax.broadcasted_iota(I32, (n, LANES), 0)

    def shifted(a, d):
        r = pltpu.roll(a, (-d) % n, 0)
        ok = (t + d < n) if d > 0 else (t + d >= 0)
        return jnp.where(ok, r, 0.0)

    for gi, w in enumerate(POOL_WINDOWS):
        @pl.when(g == gi)
        def _(w=w):
            half = w // 2
            cnt = (jnp.minimum(t + half, n) - jnp.maximum(t - half, 0)).astype(F32)
            for c0 in range(0, gw, LANES):
                uf = u_ref[:, c0:c0 + LANES].astype(F32)
                fwd = uf
                bwd = shifted(uf, -1)
                span = 1
                while span < half:
                    fwd = fwd + shifted(fwd, span)
                    bwd = bwd + shifted(bwd, -span)
                    span *= 2
                d_scr[:, c0:c0 + LANES] = ((fwd + bwd) / cnt - uf).astype(BF16)

    y = jnp.dot(d_scr[...], w_ref[...].astype(BF16), preferred_element_type=F32)
    o_ref[...] = (y * sc_ref[...]).astype(BF16)


def _pool(z, w_pool, pool_scale, col0):
    b, s, _ = z.shape
    ng, gw = w_pool.shape[1], w_pool.shape[2]
    return pl.pallas_call(
        _pool_kernel,
        grid=(b, ng),
        in_specs=[
            pl.BlockSpec((None, s, gw), lambda bi, g: (bi, 0, col0 // gw + g)),
            pl.BlockSpec((None, None, gw, gw), lambda bi, g: (0, g, 0, 0)),
            pl.BlockSpec((None, 1, gw), lambda bi, g: (g, 0, 0)),
        ],
        out_specs=pl.BlockSpec((None, s, gw), lambda bi, g: (bi, 0, g)),
        out_shape=jax.ShapeDtypeStruct((b, s, ng * gw), BF16),
        scratch_shapes=[pltpu.VMEM((s, gw), BF16)],
        compiler_params=_params("arbitrary", "arbitrary"),
    )(z, w_pool, pool_scale.reshape(ng, 1, gw))


def _outproj_kernel(a_ref, p_ref, wa_ref, wp_ref, x_ref, mod_ref, o_ref):
    y = jnp.dot(a_ref[...], wa_ref[...], preferred_element_type=F32)
    y = y + jnp.dot(p_ref[...], wp_ref[...], preferred_element_type=F32)
    o_ref[...] = x_ref[...] + mod_ref[2:3, :] * y


def _outproj(attn, pool, w_out, x, mod3):
    b, s, d = x.shape
    aw, pw = attn.shape[2], pool.shape[2]
    tm = min(1024, s)
    tn = min(512, d)
    return pl.pallas_call(
        _outproj_kernel,
        grid=(b, s // tm, d // tn),
        in_specs=[
            pl.BlockSpec((None, tm, aw), lambda bi, i, j: (bi, i, 0)),
            pl.BlockSpec((None, tm, pw), lambda bi, i, j: (bi, i, 0)),
            pl.BlockSpec((None, aw, tn), lambda bi, i, j: (0, 0, j)),
            pl.BlockSpec((None, pw, tn), lambda bi, i, j: (0, aw // pw, j)),
            pl.BlockSpec((None, tm, tn), lambda bi, i, j: (bi, i, j)),
            pl.BlockSpec((None, N_MOD, tn), lambda bi, i, j: (bi, 0, j)),
        ],
        out_specs=pl.BlockSpec((None, tm, tn), lambda bi, i, j: (bi, i, j)),
        out_shape=jax.ShapeDtypeStruct((b, s, d), F32),
        compiler_params=_params("arbitrary", "arbitrary", "arbitrary"),
    )(attn, pool, w_out, w_out, x, mod3)


def _router_kernel(x_ref, mod_ref, g_ref, wr_ref, br_ref, hp_ref, mi_ref, mg_ref, cnt_ref, carry_scr,
                   *, tm, ne):
    first = (pl.program_id(0) == 0) & (pl.program_id(1) == 0)

    @pl.when(first)
    def _():
        carry_scr[...] = jnp.zeros_like(carry_scr)

    xf = x_ref[...]
    ms = jnp.mean(xf * xf, axis=-1, keepdims=True)
    y = xf * lax.rsqrt(ms + EPS) * g_ref[...]
    h = y * (1 + mod_ref[4:5, :]) + mod_ref[3:4, :]
    half = h.shape[1] // 2
    hp_ref[...] = _pack_bf16_pair(h[:, :half], h[:, half:])
    logits = jnp.dot(h.astype(BF16), wr_ref[...].astype(BF16), preferred_element_type=F32) + br_ref[...]

    lane = lax.broadcasted_iota(I32, logits.shape, 1).astype(F32)
    work = logits
    chosen = jnp.zeros(logits.shape, jnp.bool_)
    hits, idxs, vals = [], [], []
    for _ in range(TOP_K):
        m = jnp.max(work, axis=-1, keepdims=True)
        ik = jnp.min(jnp.where(work == m, lane, float(ne)), axis=-1, keepdims=True)
        hit = lane == ik
        hits.append(hit)
        idxs.append(ik.astype(I32))
        vals.append(m)
        chosen = chosen | hit
        work = jnp.where(hit, -jnp.inf, work)

    exps = [jnp.exp(v - vals[0]) for v in vals]
    den = exps[0]
    for e in exps[1:]:
        den = den + e
    gates = [e / den for e in exps]

    sel = chosen.astype(F32)
    r = lax.broadcasted_iota(I32, (tm, tm), 0)
    c = lax.broadcasted_iota(I32, (tm, tm), 1)
    tri = (c < r).astype(BF16)
    rank = carry_scr[...] + jnp.dot(tri, sel.astype(BF16), preferred_element_type=F32)
    carry_scr[...] = carry_scr[...] + jnp.sum(sel, axis=0, keepdims=True)
    cnt_ref[...] = carry_scr[...]
    ranks = [jnp.sum(jnp.where(hit, rank, 0.0), axis=-1, keepdims=True).astype(I32) for hit in hits]

    lane128 = lax.broadcasted_iota(I32, (tm, LANES), 1)
    mi = jnp.zeros((tm, LANES), I32)
    mg = jnp.zeros((tm, LANES), F32)
    for k in range(TOP_K):
        mi = jnp.where(lane128 == k, idxs[k], mi)
        mi = jnp.where(lane128 == TOP_K + k, ranks[k], mi)
        mg = jnp.where(lane128 == k, gates[k], mg)
    mi_ref[...] = mi
    mg_ref[...] = mg


def _router(x1, mod3, norm_g, w_router, b_router):
    b, s, d = x1.shape
    ne = w_router.shape[2]
    tm = min(256, s)
    nt = s // tm
    n = b * s
    kern = functools.partial(_router_kernel, tm=tm, ne=ne)
    return pl.pallas_call(
        kern,
        grid=(b, nt),
        in_specs=[
            pl.BlockSpec((None, tm, d), lambda bi, i: (bi, i, 0)),
            pl.BlockSpec((None, N_MOD, d), lambda bi, i: (bi, 0, 0)),
            pl.BlockSpec((1, d), lambda bi, i: (0, 0)),
            pl.BlockSpec((None, d, ne), lambda bi, i: (0, 0, 0)),
            pl.BlockSpec((1, ne), lambda bi, i: (0, 0)),
        ],
        out_specs=[
            pl.BlockSpec((tm, d // 2), lambda bi, i: (bi * nt + i, 0)),
            pl.BlockSpec((tm, LANES), lambda bi, i: (bi * nt + i, 0)),
            pl.BlockSpec((tm, LANES), lambda bi, i: (bi * nt + i, 0)),
            pl.BlockSpec((1, ne), lambda bi, i: (0, 0)),
        ],
        out_shape=[
            jax.ShapeDtypeStruct((n, d // 2), I32),
            jax.ShapeDtypeStruct((n, LANES), I32),
            jax.ShapeDtypeStruct((n, LANES), F32),
            jax.ShapeDtypeStruct((1, ne), F32),
        ],
        scratch_shapes=[pltpu.VMEM((1, ne), F32)],
        compiler_params=_params("arbitrary", "arbitrary"),
    )(x1, mod3, norm_g, w_router, b_router)


DISPATCH_WINDOW = 16
MOE_TILE_ROWS = 1024
MOE_SUB_ROWS = 256


def _dispatch_kernel(pos_ref, zstart_ref, zlen_ref, hp_ref, xs_ref, zero_scr, sem, zsem, *, td, ne):
    bits = [1 << k for k in reversed(range(SUBLANES.bit_length() - 1, MOE_SUB_ROWS.bit_length() - 1))]

    def row_copy(e, r):
        return pltpu.make_async_copy(zero_scr.at[pl.ds(0, 1)], xs_ref.at[pl.ds(zstart_ref[e] + r, 1)], zsem)

    def chunk_copy(e, head, bit):
        off = pl.multiple_of(zstart_ref[e] + head + ((zlen_ref[e] - head) & ~(2 * bit - 1)), SUBLANES)
        return pltpu.make_async_copy(zero_scr.at[pl.ds(0, bit)], xs_ref.at[pl.ds(off, bit)], zsem)

    def zero_pass(start):
        def per_expert(e, carry):
            head = (-zstart_ref[e]) & (SUBLANES - 1)
            for r in range(SUBLANES - 1):
                @pl.when(r < head)
                def _(r=r):
                    row_copy(e, r).start() if start else row_copy(e, r).wait()
            for bit in bits:
                @pl.when(((zlen_ref[e] - head) & bit) != 0)
                def _(bit=bit):
                    chunk_copy(e, head, bit).start() if start else chunk_copy(e, head, bit).wait()
            return carry
        lax.fori_loop(0, ne, per_expert, 0)

    @pl.when(pl.program_id(0) == 0)
    def _():
        zero_scr[...] = jnp.zeros_like(zero_scr)
        zero_pass(True)
        zero_pass(False)

    base = pl.program_id(0) * (td * TOP_K)

    def copy(t, k):
        p = pos_ref[base + t * TOP_K + k]
        return pltpu.make_async_copy(hp_ref.at[pl.ds(t, 1)], xs_ref.at[pl.ds(p, 1)], sem)

    def body(t, carry):
        for k in range(TOP_K):
            copy(t, k).start()

        @pl.when(t >= DISPATCH_WINDOW)
        def _():
            for k in range(TOP_K):
                copy(t - DISPATCH_WINDOW, k).wait()

        return carry

    lax.fori_loop(0, td, body, 0)

    def drain(t, carry):
        for k in range(TOP_K):
            copy(t, k).wait()
        return carry

    lax.fori_loop(td - DISPATCH_WINDOW, td, drain, 0)


def _dispatch(pos_flat, zstart, zlen, hp, rows):
    n, w = hp.shape
    td = min(512, n)
    kern = functools.partial(_dispatch_kernel, td=td, ne=zstart.shape[0])
    return pl.pallas_call(
        kern,
        grid_spec=pltpu.PrefetchScalarGridSpec(
            num_scalar_prefetch=3,
            grid=(n // td,),
            in_specs=[pl.BlockSpec((td, w), lambda i, pos, zs, zl: (i, 0))],
            out_specs=pl.BlockSpec(memory_space=pl.ANY),
            scratch_shapes=[pltpu.VMEM((MOE_SUB_ROWS // 2, w), I32), pltpu.SemaphoreType.DMA,
                            pltpu.SemaphoreType.DMA],
        ),
        out_shape=jax.ShapeDtypeStruct((rows, w), I32),
        compiler_params=_params("arbitrary"),
    )(pos_flat, zstart, zlen, hp)


def _moe_up_kernel(te_ref, tr_ref, nu_ref, xs_ref, wg_ref, bg_ref, wu_ref, bu_ref, hid_ref,
                   xb_scr, wg_scr, wu_scr, *, sub):
    i = pl.program_id(0)
    j = pl.program_id(1)
    rows = tr_ref[i]
    half = xs_ref.shape[1]

    @pl.when(rows > 0)
    def _():
        wg_scr[...] = wg_ref[...].astype(BF16)
        wu_scr[...] = wu_ref[...].astype(BF16)

    for s in range(xs_ref.shape[0] // sub):
        blk = pl.ds(s * sub, sub)

        @pl.when(s * sub < rows)
        def _(blk=blk):
            @pl.when(j == 0)
            def _():
                lo, hi = _unpack_bf16_pair(xs_ref[blk, :])
                xb_scr[blk, :half] = lo.astype(BF16)
                xb_scr[blk, half:] = hi.astype(BF16)

            xb = xb_scr[blk, :]
            a = jnp.dot(xb, wg_scr[...], preferred_element_type=F32) + bg_ref[...]
            u = jnp.dot(xb, wu_scr[...], preferred_element_type=F32) + bu_ref[...]
            a = jnp.minimum(a, SWIGLU_LIMIT)
            u = jnp.clip(u, -SWIGLU_LIMIT, SWIGLU_LIMIT)
            hid_ref[blk, :] = ((u + 1) * (a * jax.nn.sigmoid(SWIGLU_ALPHA * a))).astype(BF16)

        @pl.when(s * sub >= rows)
        def _(blk=blk):
            hid_ref[blk, :] = jnp.zeros((sub, hid_ref.shape[1]), BF16)


def _moe_index_helpers(n_col_blocks):
    def row(i, nu):
        return jnp.minimum(i, nu[0] - 1)

    def col(i, j, nu):
        return jnp.where(i < nu[0], j, n_col_blocks - 1)

    return row, col


def _moe_up(tile_expert, tile_rows, n_used, xs, w_gate, b_gate, w_up, b_up, tm, sub):
    rows, half = xs.shape
    d = 2 * half
    ne, ff = w_gate.shape[1], w_gate.shape[3]
    tf = min(256, ff)
    nf = ff // tf
    row, col = _moe_index_helpers(nf)
    w_spec = pl.BlockSpec((None, None, d, tf), lambda i, j, te, tr, nu: (0, te[row(i, nu)], 0, col(i, j, nu)))
    b_spec = pl.BlockSpec((None, 1, tf), lambda i, j, te, tr, nu: (te[row(i, nu)], 0, col(i, j, nu)))
    return pl.pallas_call(
        functools.partial(_moe_up_kernel, sub=sub),
        grid_spec=pltpu.PrefetchScalarGridSpec(
            num_scalar_prefetch=3,
            grid=(rows // tm, nf),
            in_specs=[
                pl.BlockSpec((tm, half), lambda i, j, te, tr, nu: (row(i, nu), 0)),
                w_spec, b_spec, w_spec, b_spec,
            ],
            out_specs=pl.BlockSpec((tm, tf), lambda i, j, te, tr, nu: (i, j)),
            scratch_shapes=[pltpu.VMEM((tm, d), BF16), pltpu.VMEM((d, tf), BF16), pltpu.VMEM((d, tf), BF16)],
        ),
        out_shape=jax.ShapeDtypeStruct((rows, ff), BF16),
        compiler_params=_params("arbitrary", "arbitrary"),
    )(tile_expert, tile_rows, n_used, xs, w_gate, b_gate.reshape(ne, 1, ff), w_up, b_up.reshape(ne, 1, ff))


def _moe_down_kernel(te_ref, tr_ref, nu_ref, hid_ref, wlo_ref, whi_ref, blo_ref, bhi_ref, ys_ref,
                     wlo_scr, whi_scr, *, sub):
    rows = tr_ref[pl.program_id(0)]

    @pl.when(rows > 0)
    def _():
        wlo_scr[...] = wlo_ref[...].astype(BF16)
        whi_scr[...] = whi_ref[...].astype(BF16)

    for s in range(hid_ref.shape[0] // sub):
        blk = pl.ds(s * sub, sub)

        @pl.when(s * sub < rows)
        def _(blk=blk):
            h = hid_ref[blk, :]
            ylo = jnp.dot(h, wlo_scr[...], preferred_element_type=F32) + blo_ref[...]
            yhi = jnp.dot(h, whi_scr[...], preferred_element_type=F32) + bhi_ref[...]
            ys_ref[blk, :] = _pack_bf16_pair(ylo, yhi)

        @pl.when(s * sub >= rows)
        def _(blk=blk):
            ys_ref[blk, :] = jnp.zeros((sub, ys_ref.shape[1]), I32)


def _moe_down(tile_expert, tile_rows, n_used, hid, w_down, b_down, tm, sub):
    rows, ff = hid.shape
    ne, d = w_down.shape[1], w_down.shape[3]
    half = d // 2
    tn = min(512, half)
    nj = half // tn
    row, col = _moe_index_helpers(nj)

    def w_spec(off):
        return pl.BlockSpec((None, None, ff, tn),
                            lambda i, j, te, tr, nu: (0, te[row(i, nu)], 0, col(i, j, nu) + off))

    def b_spec(off):
        return pl.BlockSpec((None, 1, tn), lambda i, j, te, tr, nu: (te[row(i, nu)], 0, col(i, j, nu) + off))

    return pl.pallas_call(
        functools.partial(_moe_down_kernel, sub=sub),
        grid_spec=pltpu.PrefetchScalarGridSpec(
            num_scalar_prefetch=3,
            grid=(rows // tm, nj),
            in_specs=[
                pl.BlockSpec((tm, ff), lambda i, j, te, tr, nu: (row(i, nu), 0)),
                w_spec(0), w_spec(nj), b_spec(0), b_spec(nj),
            ],
            out_specs=pl.BlockSpec((tm, tn), lambda i, j, te, tr, nu: (i, j)),
            scratch_shapes=[pltpu.VMEM((ff, tn), BF16), pltpu.VMEM((ff, tn), BF16)],
        ),
        out_shape=jax.ShapeDtypeStruct((rows, half), I32),
        compiler_params=_params("arbitrary", "arbitrary"),
    )(tile_expert, tile_rows, n_used, hid, w_down, w_down, b_down.reshape(ne, 1, d), b_down.reshape(ne, 1, d))


def _combine_kernel(pos_ref, ys_ref, x_ref, mg_ref, mod_ref, fg_ref, o_ref, ybuf, sem, *, tc, nt):
    base = (pl.program_id(0) * nt + pl.program_id(1)) * (tc * TOP_K)

    def copy(t, k):
        p = pos_ref[base + t * TOP_K + k]
        return pltpu.make_async_copy(ys_ref.at[pl.ds(p, 1)], ybuf.at[k, pl.ds(t, 1)], sem)

    def issue(t, carry):
        for k in range(TOP_K):
            copy(t, k).start()
        return carry

    def wait(t, carry):
        for k in range(TOP_K):
            copy(t, k).wait()
        return carry

    lax.fori_loop(0, tc, issue, 0)
    lax.fori_loop(0, tc, wait, 0)

    half = ybuf.shape[2]
    acc_lo = jnp.zeros((tc, half), F32)
    acc_hi = jnp.zeros((tc, half), F32)
    for k in range(TOP_K):
        lo, hi = _unpack_bf16_pair(ybuf[k])
        g = mg_ref[:, k:k + 1]
        acc_lo = acc_lo + lo * g
        acc_hi = acc_hi + hi * g
    x_lo = x_ref[:, :half] + mod_ref[5:6, :half] * acc_lo
    x_hi = x_ref[:, half:] + mod_ref[5:6, half:] * acc_hi
    ms = (jnp.sum(x_lo * x_lo, axis=-1, keepdims=True) + jnp.sum(x_hi * x_hi, axis=-1, keepdims=True)) / (2 * half)
    inv = lax.rsqrt(ms + EPS)
    o_ref[:, :half] = x_lo * inv * fg_ref[:, :half]
    o_ref[:, half:] = x_hi * inv * fg_ref[:, half:]


def _combine(pos_flat, ys, x1, mg, mod3, final_g):
    b, s, d = x1.shape
    tc = min(256, s)
    nt = s // tc
    kern = functools.partial(_combine_kernel, tc=tc, nt=nt)
    return pl.pallas_call(
        kern,
        grid_spec=pltpu.PrefetchScalarGridSpec(
            num_scalar_prefetch=1,
            grid=(b, nt),
            in_specs=[
                pl.BlockSpec(memory_space=pl.ANY),
                pl.BlockSpec((None, tc, d), lambda bi, i, pos: (bi, i, 0)),
                pl.BlockSpec((tc, LANES), lambda bi, i, pos: (bi * nt + i, 0)),
                pl.BlockSpec((None, N_MOD, d), lambda bi, i, pos: (bi, 0, 0)),
                pl.BlockSpec((1, d), lambda bi, i, pos: (0, 0)),
            ],
            out_specs=pl.BlockSpec((None, tc, d), lambda bi, i, pos: (bi, i, 0)),
            scratch_shapes=[pltpu.VMEM((TOP_K, tc, d // 2), I32), pltpu.SemaphoreType.DMA],
        ),
        out_shape=jax.ShapeDtypeStruct((b, s, d), F32),
        compiler_params=_params("arbitrary", "arbitrary"),
    )(pos_flat, ys, x1, mg, mod3, final_g)


def _rope_tables(seq_len):
    pairs = HEAD_DIM // 4
    rows = seq_len // GRID_W
    row = jnp.repeat(jnp.arange(rows, dtype=F32), GRID_W)
    col = jnp.tile(jnp.arange(GRID_W, dtype=F32), rows)
    freqs = ROPE_THETA ** (-jnp.arange(pairs, dtype=F32) / pairs)
    ang = jnp.concatenate([row[:, None] * freqs, col[:, None] * freqs], axis=-1)
    cos, sin = jnp.cos(ang), jnp.sin(ang)
    return jnp.concatenate([cos, cos], axis=-1), jnp.concatenate([-sin, sin], axis=-1)


def _routing_tables(counts, n_assign, tm, sub):
    ne = counts.shape[0]
    padded = (counts + tm - 1) // tm * tm
    pend = jnp.cumsum(padded)
    pstart = pend - padded
    n_tiles = n_assign // tm + ne
    tile_row0 = jnp.arange(n_tiles, dtype=I32) * tm
    tile_expert = jnp.minimum(jnp.sum((pend[None, :] <= tile_row0[:, None]).astype(I32), axis=1), ne - 1)
    tile_rows = jnp.clip(counts[tile_expert] - (tile_row0 - pstart[tile_expert]), 0, tm)
    tile_rows = jnp.where(tile_row0 < pend[-1], tile_rows, 0).astype(I32)
    n_used = (pend[-1:] // tm).astype(I32)
    zstart = (pstart + counts).astype(I32)
    zlen = ((counts + sub - 1) // sub * sub - counts).astype(I32)
    return pstart, n_tiles, tile_expert.astype(I32), tile_rows, n_used, zstart, zlen


def kernel(x, c, ctx, c_ctx, w_ada, b_ada, norm1_g, w_in, q_norm_g, k_norm_g, w_pool, pool_scale, w_out,
           norm2_g, w_router, b_router, w_gate, b_gate, w_up, b_up, w_down, b_down, final_g):
    b, s, d = x.shape
    assert w_ada.shape[0] == 1, "single-layer trunk"
    assert b + 1 <= MOD_ROWS
    pool_w = d // 4
    attn_w = d - pool_w
    n_kv = attn_w // HEAD_DIM // Q_PER_KV
    kv_w = n_kv * HEAD_DIM
    ne = w_router.shape[2]

    cc = jnp.concatenate([c, c_ctx[None, :], jnp.zeros((MOD_ROWS - b - 1, d), F32)], axis=0)
    mod3 = _adaln(cc, w_ada, b_ada).reshape(MOD_ROWS, N_MOD, d)

    cosf, sinf = _rope_tables(s)
    w_in = w_in.astype(BF16)
    w_out = w_out.astype(BF16)
    q_gain = q_norm_g * (ATTN_SCALE * LOG2_E)
    z = _inproj(x, mod3, None, norm1_g, w_in, 0, attn_w + 2 * kv_w + pool_w, attn_w, kv_w,
                q_gain, k_norm_g, cosf, sinf, True)
    zc = _inproj(ctx, mod3, b, norm1_g, w_in, attn_w, 2 * kv_w, 0, kv_w,
                 q_norm_g, k_norm_g, cosf, sinf, False)
    attn = _attention(z, zc, n_kv)
    pool = _pool(z, w_pool, pool_scale, attn_w + 2 * kv_w)
    x1 = _outproj(attn, pool, w_out, x, mod3)

    hp, mi, mg, counts = _router(x1, mod3, norm2_g, w_router, b_router)
    n = b * s
    tm = min(MOE_TILE_ROWS, n)
    sub = min(MOE_SUB_ROWS, tm)
    pstart, n_tiles, tile_expert, tile_rows, n_used, zstart, zlen = _routing_tables(
        counts[0].astype(I32), n * TOP_K, tm, sub)
    pos_flat = (pstart[mi[:, :TOP_K]] + mi[:, TOP_K:2 * TOP_K]).reshape(n * TOP_K)

    xs = _dispatch(pos_flat, zstart, zlen, hp, n_tiles * tm)
    hid = _moe_up(tile_expert, tile_rows, n_used, xs, w_gate, b_gate[0], w_up, b_up[0], tm, sub)
    ys = _moe_down(tile_expert, tile_rows, n_used, hid, w_down, b_down[0], tm, sub)
    return _combine(pos_flat, ys, x1, mg, mod3, final_g[None, :])
```

```python
import functools

import jax
import jax.numpy as jnp
from jax import lax
from jax.experimental import pallas as pl
from jax.experimental.pallas import tpu as pltpu

F32 = jnp.float32
BF16 = jnp.bfloat16
I32 = jnp.int32

EPS = 1e-6
HEAD_DIM = 128
Q_PER_KV = 3
GRID_W = 64
ROPE_THETA = 10000.0
POOL_WINDOWS = (2, 4, 8, 16)
TOP_K = 4
SWIGLU_ALPHA = 1.702
SWIGLU_LIMIT = 7.0
N_MOD = 6
ATTN_SCALE = HEAD_DIM ** -0.5
LOG2_E = 1.4426950408889634

V7X_VMEM_BYTES = 64 * 1024 * 1024
VMEM_LIMIT = V7X_VMEM_BYTES * 7 // 8
LANES = 128
SUBLANES = 8
ATTN_ROW_BLOCK = 128
NORM_ROW_CHUNK = 256
MOD_ROWS = 16
HI_MASK = -65536


def _params(*sem):
    return pltpu.CompilerParams(dimension_semantics=sem, vmem_limit_bytes=VMEM_LIMIT)


def _pack_bf16_pair(lo, hi):
    lo_bits = lax.bitcast_convert_type(lo.astype(BF16).astype(F32), I32)
    hi_bits = lax.bitcast_convert_type(hi.astype(BF16).astype(F32), I32)
    return lax.shift_right_logical(lo_bits, 16) | hi_bits


def _unpack_bf16_pair(p):
    lo = lax.bitcast_convert_type(lax.shift_left(p, 16), F32)
    hi = lax.bitcast_convert_type(p & HI_MASK, F32)
    return lo, hi


def _adaln_kernel(c_ref, w_ref, b_ref, o_ref):
    c = c_ref[...]
    a = (c * jax.nn.sigmoid(c)).astype(BF16)
    o_ref[...] = jnp.dot(a, w_ref[...].astype(BF16), preferred_element_type=F32) + b_ref[...]


def _adaln(cc, w_ada, b_ada):
    d = cc.shape[1]
    n = w_ada.shape[2]
    tn = min(512, n)
    return pl.pallas_call(
        _adaln_kernel,
        grid=(n // tn,),
        in_specs=[
            pl.BlockSpec((MOD_ROWS, d), lambda j: (0, 0)),
            pl.BlockSpec((None, d, tn), lambda j: (0, 0, j)),
            pl.BlockSpec((1, tn), lambda j: (0, j)),
        ],
        out_specs=pl.BlockSpec((MOD_ROWS, tn), lambda j: (0, j)),
        out_shape=jax.ShapeDtypeStruct((MOD_ROWS, n), F32),
        compiler_params=_params("arbitrary"),
    )(cc, w_ada, b_ada)


def _inproj_kernel(x_ref, mod_ref, g_ref, w_ref, qg_ref, kg_ref, cos_ref, sin_ref, o_ref, h_scr,
                   *, nq, nk, rope, heads_per_tile):
    j = pl.program_id(2)

    @pl.when(j == 0)
    def _():
        rc = min(NORM_ROW_CHUNK, x_ref.shape[0])

        def chunk(r, carry):
            rows = pl.ds(pl.multiple_of(r * rc, rc), rc)
            xf = x_ref[rows, :]
            ms = jnp.mean(xf * xf, axis=-1, keepdims=True)
            y = xf * lax.rsqrt(ms + EPS) * g_ref[...]
            h_scr[rows, :] = (y * (1 + mod_ref[1:2, :]) + mod_ref[0:1, :]).astype(BF16)
            return carry

        lax.fori_loop(0, x_ref.shape[0] // rc, chunk, 0)

    z = jnp.dot(h_scr[...], w_ref[...], preferred_element_type=F32)

    def norm_rope(g):
        for hh in range(heads_per_tile):
            zc = z[:, hh * HEAD_DIM:(hh + 1) * HEAD_DIM]
            ms = jnp.mean(zc * zc, axis=-1, keepdims=True)
            zn = zc * lax.rsqrt(ms + EPS) * g
            if rope:
                zn = zn * cos_ref[...] + pltpu.roll(zn, HEAD_DIM // 2, 1) * sin_ref[...]
            o_ref[:, hh * HEAD_DIM:(hh + 1) * HEAD_DIM] = zn.astype(BF16)

    if nq:
        @pl.when(j < nq)
        def _():
            norm_rope(qg_ref[...])

    @pl.when((j >= nq) & (j < nq + nk))
    def _():
        norm_rope(kg_ref[...])

    @pl.when(j >= nq + nk)
    def _():
        o_ref[...] = z.astype(BF16)


def _inproj(x, mod3, mod_row, norm_g, w_in, col0, ncols, nq_cols, nk_cols, q_g, k_g, cosf, sinf, rope):
    b, l, d = x.shape
    tm = min(1024, l)
    tn = min(512, ncols)
    kern = functools.partial(_inproj_kernel, nq=nq_cols // tn, nk=nk_cols // tn, rope=rope,
                             heads_per_tile=tn // HEAD_DIM)
    if mod_row is None:
        mod_map = lambda bi, i, j: (bi, 0, 0)
    else:
        mod_map = lambda bi, i, j: (mod_row, 0, 0)
    return pl.pallas_call(
        kern,
        grid=(b, l // tm, ncols // tn),
        in_specs=[
            pl.BlockSpec((None, tm, d), lambda bi, i, j: (bi, i, 0), pipeline_mode=pl.Buffered(1)),
            pl.BlockSpec((None, N_MOD, d), mod_map),
            pl.BlockSpec((1, d), lambda bi, i, j: (0, 0)),
            pl.BlockSpec((None, d, tn), lambda bi, i, j: (0, 0, j + col0 // tn)),
            pl.BlockSpec((1, HEAD_DIM), lambda bi, i, j: (0, 0)),
            pl.BlockSpec((1, HEAD_DIM), lambda bi, i, j: (0, 0)),
            pl.BlockSpec((tm, HEAD_DIM), lambda bi, i, j: (i, 0)),
            pl.BlockSpec((tm, HEAD_DIM), lambda bi, i, j: (i, 0)),
        ],
        out_specs=pl.BlockSpec((None, tm, tn), lambda bi, i, j: (bi, i, j)),
        out_shape=jax.ShapeDtypeStruct((b, l, ncols), BF16),
        scratch_shapes=[pltpu.VMEM((tm, d), BF16)],
        compiler_params=_params("arbitrary", "arbitrary", "arbitrary"),
    )(x, mod3, norm_g, w_in, q_g, k_g, cosf, sinf)


def _attn_kernel(q_ref, kc_ref, vc_ref, k_ref, v_ref, o_ref):
    nt = (((1,), (1,)), ((), ()))
    vc = jnp.concatenate([vc_ref[...], jnp.ones(vc_ref.shape, BF16)], axis=1)
    vl = jnp.concatenate([v_ref[...], jnp.ones(v_ref.shape, BF16)], axis=1)
    tq = q_ref.shape[0]
    rb = min(ATTN_ROW_BLOCK, tq)
    blocks = [(r0, g * HEAD_DIM) for g in range(Q_PER_KV) for r0 in range(0, tq, rb)]

    def scores(blk):
        r0, c0 = blk
        q = q_ref[r0:r0 + rb, c0:c0 + HEAD_DIM]
        return (lax.dot_general(q, kc_ref[...], nt, preferred_element_type=F32),
                lax.dot_general(q, k_ref[...], nt, preferred_element_type=F32))

    nxt = scores(blocks[0])
    for bi, (r0, c0) in enumerate(blocks):
        sc, sl = nxt
        if bi + 1 < len(blocks):
            nxt = scores(blocks[bi + 1])
        m = jnp.maximum(jnp.max(sc, axis=-1, keepdims=True), jnp.max(sl, axis=-1, keepdims=True))
        pc = jnp.exp2((sc - m).astype(BF16))
        pk = jnp.exp2((sl - m).astype(BF16))
        o = jnp.dot(pc, vc, preferred_element_type=F32) + jnp.dot(pk, vl, preferred_element_type=F32)
        o_ref[r0:r0 + rb, c0:c0 + HEAD_DIM] = (o[:, :HEAD_DIM] / o[:, HEAD_DIM:HEAD_DIM + 1]).astype(BF16)


def _attention(z, zc, n_kv):
    b, s, _ = z.shape
    c = zc.shape[1]
    tq = min(512, s)
    gw = Q_PER_KV * HEAD_DIM
    n_q = n_kv * Q_PER_KV
    return pl.pallas_call(
        _attn_kernel,
        grid=(b, n_kv, s // tq),
        in_specs=[
            pl.BlockSpec((None, tq, gw), lambda bi, g, i: (bi, i, g)),
            pl.BlockSpec((None, c, HEAD_DIM), lambda bi, g, i: (bi, 0, g)),
            pl.BlockSpec((None, c, HEAD_DIM), lambda bi, g, i: (bi, 0, n_kv + g)),
            pl.BlockSpec((None, s, HEAD_DIM), lambda bi, g, i: (bi, 0, n_q + g)),
            pl.BlockSpec((None, s, HEAD_DIM), lambda bi, g, i: (bi, 0, n_q + n_kv + g)),
        ],
        out_specs=pl.BlockSpec((None, tq, gw), lambda bi, g, i: (bi, i, g)),
        out_shape=jax.ShapeDtypeStruct((b, s, n_q * HEAD_DIM), BF16),
        compiler_params=_params("arbitrary", "arbitrary", "arbitrary"),
    )(z, zc, zc, z, z)


def _pool_kernel(u_ref, w_ref, sc_ref, o_ref, d_scr):
    g = pl.program_id(1)
    n, gw = u_ref.shape
    t = lax.broadcasted_iota(I32, (n, LANES), 0)

    def shifted(a, d):
        r = pltpu.roll(a, (-d) % n, 0)
        ok = (t + d < n) if d > 0 else (t + d >= 0)
        return jnp.where(ok, r, 0.0)

    for gi, w in enumerate(POOL_WINDOWS):
        @pl.when(g == gi)
        def _(w=w):
            half = w // 2
            cnt = (jnp.minimum(t + half, n) - jnp.maximum(t - half, 0)).astype(F32)
            for c0 in range(0, gw, LANES):
                uf = u_ref[:, c0:c0 + LANES].astype(F32)
                fwd = uf
                bwd = shifted(uf, -1)
                span = 1
                while span < half:
                    fwd = fwd + shifted(fwd, span)
                    bwd = bwd + shifted(bwd, -span)
                    span *= 2
                d_scr[:, c0:c0 + LANES] = ((fwd + bwd) / cnt - uf).astype(BF16)

    y = jnp.dot(d_scr[...], w_ref[...].astype(BF16), preferred_element_type=F32)
    o_ref[...] = (y * sc_ref[...]).astype(BF16)


def _pool(z, w_pool, pool_scale, col0):
    b, s, _ = z.shape
    ng, gw = w_pool.shape[1], w_pool.shape[2]
    return pl.pallas_call(
        _pool_kernel,
        grid=(b, ng),
        in_specs=[
            pl.BlockSpec((None, s, gw), lambda bi, g: (bi, 0, col0 // gw + g)),
            pl.BlockSpec((None, None, gw, gw), lambda bi, g: (0, g, 0, 0)),
            pl.BlockSpec((None, 1, gw), lambda bi, g: (g, 0, 0)),
        ],
        out_specs=pl.BlockSpec((None, s, gw), lambda bi, g: (bi, 0, g)),
        out_shape=jax.ShapeDtypeStruct((b, s, ng * gw), BF16),
        scratch_shapes=[pltpu.VMEM((s, gw), BF16)],
        compiler_params=_params("arbitrary", "arbitrary"),
    )(z, w_pool, pool_scale.reshape(ng, 1, gw))


def _outproj_kernel(a_ref, p_ref, wa_ref, wp_ref, x_ref, mod_ref, o_ref):
    y = jnp.dot(a_ref[...], wa_ref[...], preferred_element_type=F32)
    y = y + jnp.dot(p_ref[...], wp_ref[...], preferred_element_type=F32)
    o_ref[...] = x_ref[...] + mod_ref[2:3, :] * y


def _outproj(attn, pool, w_out, x, mod3):
    b, s, d = x.shape
    aw, pw = attn.shape[2], pool.shape[2]
    tm = min(1024, s)
    tn = min(512, d)
    return pl.pallas_call(
        _outproj_kernel,
        grid=(b, s // tm, d // tn),
        in_specs=[
            pl.BlockSpec((None, tm, aw), lambda bi, i, j: (bi, i, 0)),
            pl.BlockSpec((None, tm, pw), lambda bi, i, j: (bi, i, 0)),
            pl.BlockSpec((None, aw, tn), lambda bi, i, j: (0, 0, j)),
            pl.BlockSpec((None, pw, tn), lambda bi, i, j: (0, aw // pw, j)),
            pl.BlockSpec((None, tm, tn), lambda bi, i, j: (bi, i, j)),
            pl.BlockSpec((None, N_MOD, tn), lambda bi, i, j: (bi, 0, j)),
        ],
        out_specs=pl.BlockSpec((None, tm, tn), lambda bi, i, j: (bi, i, j)),
        out_shape=jax.ShapeDtypeStruct((b, s, d), F32),
        compiler_params=_params("arbitrary", "arbitrary", "arbitrary"),
    )(attn, pool, w_out, w_out, x, mod3)


def _router_kernel(x_ref, mod_ref, g_ref, wr_ref, br_ref, hp_ref, mi_ref, mg_ref, cnt_ref, carry_scr,
                   *, tm, ne):
    first = (pl.program_id(0) == 0) & (pl.program_id(1) == 0)

    @pl.when(first)
    def _():
        carry_scr[...] = jnp.zeros_like(carry_scr)

    xf = x_ref[...]
    ms = jnp.mean(xf * xf, axis=-1, keepdims=True)
    y = xf * lax.rsqrt(ms + EPS) * g_ref[...]
    h = y * (1 + mod_ref[4:5, :]) + mod_ref[3:4, :]
    half = h.shape[1] // 2
    hp_ref[...] = _pack_bf16_pair(h[:, :half], h[:, half:])
    logits = jnp.dot(h.astype(BF16), wr_ref[...].astype(BF16), preferred_element_type=F32) + br_ref[...]

    lane = lax.broadcasted_iota(I32, logits.shape, 1).astype(F32)
    work = logits
    chosen = jnp.zeros(logits.shape, jnp.bool_)
    hits, idxs, vals = [], [], []
    for _ in range(TOP_K):
        m = jnp.max(work, axis=-1, keepdims=True)
        ik = jnp.min(jnp.where(work == m, lane, float(ne)), axis=-1, keepdims=True)
        hit = lane == ik
        hits.append(hit)
        idxs.append(ik.astype(I32))
        vals.append(m)
        chosen = chosen | hit
        work = jnp.where(hit, -jnp.inf, work)

    exps = [jnp.exp(v - vals[0]) for v in vals]
    den = exps[0]
    for e in exps[1:]:
        den = den + e
    gates = [e / den for e in exps]

    sel = chosen.astype(F32)
    r = lax.broadcasted_iota(I32, (tm, tm), 0)
    c = lax.broadcasted_iota(I32, (tm, tm), 1)
    tri = (c < r).astype(BF16)
    rank = carry_scr[...] + jnp.dot(tri, sel.astype(BF16), preferred_element_type=F32)
    carry_scr[...] = carry_scr[...] + jnp.sum(sel, axis=0, keepdims=True)
    cnt_ref[...] = carry_scr[...]
    ranks = [jnp.sum(jnp.where(hit, rank, 0.0), axis=-1, keepdims=True).astype(I32) for hit in hits]

    lane128 = lax.broadcasted_iota(I32, (tm, LANES), 1)
    mi = jnp.zeros((tm, LANES), I32)
    mg = jnp.zeros((tm, LANES), F32)
    for k in range(TOP_K):
        mi = jnp.where(lane128 == k, idxs[k], mi)
        mi = jnp.where(lane128 == TOP_K + k, ranks[k], mi)
        mg = jnp.where(lane128 == k, gates[k], mg)
    mi_ref[...] = mi
    mg_ref[...] = mg


def _router(x1, mod3, norm_g, w_router, b_router):
    b, s, d = x1.shape
    ne = w_router.shape[2]
    tm = min(256, s)
    nt = s // tm
    n = b * s
    kern = functools.partial(_router_kernel, tm=tm, ne=ne)
    return pl.pallas_call(
        kern,
        grid=(b, nt),
        in_specs=[
            pl.BlockSpec((None, tm, d), lambda bi, i: (bi, i, 0)),
            pl.BlockSpec((None, N_MOD, d), lambda bi, i: (bi, 0, 0)),
            pl.BlockSpec((1, d), lambda bi, i: (0, 0)),
            pl.BlockSpec((None, d, ne), lambda bi, i: (0, 0, 0)),
            pl.BlockSpec((1, ne), lambda bi, i: (0, 0)),
        ],
        out_specs=[
            pl.BlockSpec((tm, d // 2), lambda bi, i: (bi * nt + i, 0)),
            pl.BlockSpec((tm, LANES), lambda bi, i: (bi * nt + i, 0)),
            pl.BlockSpec((tm, LANES), lambda bi, i: (bi * nt + i, 0)),
            pl.BlockSpec((1, ne), lambda bi, i: (0, 0)),
        ],
        out_shape=[
            jax.ShapeDtypeStruct((n, d // 2), I32),
            jax.ShapeDtypeStruct((n, LANES), I32),
            jax.ShapeDtypeStruct((n, LANES), F32),
            jax.ShapeDtypeStruct((1, ne), F32),
        ],
        scratch_shapes=[pltpu.VMEM((1, ne), F32)],
        compiler_params=_params("arbitrary", "arbitrary"),
    )(x1, mod3, norm_g, w_router, b_router)


DISPATCH_WINDOW = 16
MOE_TILE_ROWS = 1024
MOE_SUB_ROWS = 256
MOE_K_CHUNK = 512


def _dispatch_kernel(pos_ref, zstart_ref, zlen_ref, hp_ref, xs_ref, zero_scr, sem, zsem, *, td, ne):
    bits = [1 << k for k in reversed(range(SUBLANES.bit_length() - 1, MOE_SUB_ROWS.bit_length() - 1))]

    def row_copy(e, r):
        return pltpu.make_async_copy(zero_scr.at[pl.ds(0, 1)], xs_ref.at[pl.ds(zstart_ref[e] + r, 1)], zsem)

    def chunk_copy(e, head, bit):
        off = pl.multiple_of(zstart_ref[e] + head + ((zlen_ref[e] - head) & ~(2 * bit - 1)), SUBLANES)
        return pltpu.make_async_copy(zero_scr.at[pl.ds(0, bit)], xs_ref.at[pl.ds(off, bit)], zsem)

    def zero_pass(start):
        def per_expert(e, carry):
            head = (-zstart_ref[e]) & (SUBLANES - 1)
            for r in range(SUBLANES - 1):
                @pl.when(r < head)
                def _(r=r):
                    row_copy(e, r).start() if start else row_copy(e, r).wait()
            for bit in bits:
                @pl.when(((zlen_ref[e] - head) & bit) != 0)
                def _(bit=bit):
                    chunk_copy(e, head, bit).start() if start else chunk_copy(e, head, bit).wait()
            return carry
        lax.fori_loop(0, ne, per_expert, 0)

    @pl.when(pl.program_id(0) == 0)
    def _():
        zero_scr[...] = jnp.zeros_like(zero_scr)
        zero_pass(True)
        zero_pass(False)

    base = pl.program_id(0) * (td * TOP_K)

    def copy(t, k):
        p = pos_ref[base + t * TOP_K + k]
        return pltpu.make_async_copy(hp_ref.at[pl.ds(t, 1)], xs_ref.at[pl.ds(p, 1)], sem)

    def body(t, carry):
        for k in range(TOP_K):
            copy(t, k).start()

        @pl.when(t >= DISPATCH_WINDOW)
        def _():
            for k in range(TOP_K):
                copy(t - DISPATCH_WINDOW, k).wait()

        return carry

    lax.fori_loop(0, td, body, 0)

    def drain(t, carry):
        for k in range(TOP_K):
            copy(t, k).wait()
        return carry

    lax.fori_loop(td - DISPATCH_WINDOW, td, drain, 0)


def _dispatch(pos_flat, zstart, zlen, hp, rows):
    n, w = hp.shape
    td = min(512, n)
    kern = functools.partial(_dispatch_kernel, td=td, ne=zstart.shape[0])
    return pl.pallas_call(
        kern,
        grid_spec=pltpu.PrefetchScalarGridSpec(
            num_scalar_prefetch=3,
            grid=(n // td,),
            in_specs=[pl.BlockSpec((td, w), lambda i, pos, zs, zl: (i, 0))],
            out_specs=pl.BlockSpec(memory_space=pl.ANY),
            scratch_shapes=[pltpu.VMEM((MOE_SUB_ROWS // 2, w), I32), pltpu.SemaphoreType.DMA,
                            pltpu.SemaphoreType.DMA],
        ),
        out_shape=jax.ShapeDtypeStruct((rows, w), I32),
        compiler_params=_params("arbitrary"),
    )(pos_flat, zstart, zlen, hp)


def _chunked_dots(x_ref, m, w_refs, kc):
    d = w_refs[0].shape[0]
    cast = lambda k0: [w[k0:k0 + kc, :].astype(BF16) for w in w_refs]
    accs = [None] * len(w_refs)
    nxt = cast(0)
    for k0 in range(0, d, kc):
        cur = nxt
        if k0 + kc < d:
            nxt = cast(k0 + kc)
        xk = x_ref[0:m, k0:k0 + kc]
        for n, wk in enumerate(cur):
            part = jnp.dot(xk, wk, preferred_element_type=F32)
            accs[n] = part if accs[n] is None else accs[n] + part
    return accs


def _moe_up_kernel(te_ref, tr_ref, nu_ref, xs_ref, wg_ref, bg_ref, wu_ref, bu_ref, hid_ref, xb_scr, *, sub):
    i = pl.program_id(0)
    j = pl.program_id(1)
    rows = tr_ref[i]
    tm, half = xs_ref.shape
    n_sub = (rows + sub - 1) // sub

    for v in range(1, tm // sub + 1):
        m = v * sub

        @pl.when(n_sub == v)
        def _(m=m):
            @pl.when(j == 0)
            def _():
                lo, hi = _unpack_bf16_pair(xs_ref[0:m, :])
                xb_scr[0:m, :half] = lo.astype(BF16)
                xb_scr[0:m, half:] = hi.astype(BF16)

            a, u = _chunked_dots(xb_scr, m, (wg_ref, wu_ref), MOE_K_CHUNK)
            a = jnp.minimum(a + bg_ref[...], SWIGLU_LIMIT)
            u = jnp.clip(u + bu_ref[...], -SWIGLU_LIMIT, SWIGLU_LIMIT)
            hid_ref[0:m, :] = ((u + 1) * (a * jax.nn.sigmoid(SWIGLU_ALPHA * a))).astype(BF16)
            if m < tm:
                hid_ref[m:tm, :] = jnp.zeros((tm - m, hid_ref.shape[1]), BF16)

    @pl.when(n_sub == 0)
    def _():
        hid_ref[...] = jnp.zeros_like(hid_ref)


def _moe_index_helpers(n_col_blocks):
    def row(i, nu):
        return jnp.minimum(i, nu[0] - 1)

    def col(i, j, nu):
        return jnp.where(i < nu[0], j, n_col_blocks - 1)

    return row, col


def _moe_up(tile_expert, tile_rows, n_used, xs, w_gate, b_gate, w_up, b_up, tm, sub):
    rows, half = xs.shape
    d = 2 * half
    ne, ff = w_gate.shape[1], w_gate.shape[3]
    tf = min(256, ff)
    nf = ff // tf
    row, col = _moe_index_helpers(nf)
    w_spec = pl.BlockSpec((None, None, d, tf), lambda i, j, te, tr, nu: (0, te[row(i, nu)], 0, col(i, j, nu)))
    b_spec = pl.BlockSpec((None, 1, tf), lambda i, j, te, tr, nu: (te[row(i, nu)], 0, col(i, j, nu)))
    return pl.pallas_call(
        functools.partial(_moe_up_kernel, sub=sub),
        grid_spec=pltpu.PrefetchScalarGridSpec(
            num_scalar_prefetch=3,
            grid=(rows // tm, nf),
            in_specs=[
                pl.BlockSpec((tm, half), lambda i, j, te, tr, nu: (row(i, nu), 0)),
                w_spec, b_spec, w_spec, b_spec,
            ],
            out_specs=pl.BlockSpec((tm, tf), lambda i, j, te, tr, nu: (i, j)),
            scratch_shapes=[pltpu.VMEM((tm, d), BF16)],
        ),
        out_shape=jax.ShapeDtypeStruct((rows, ff), BF16),
        compiler_params=_params("arbitrary", "arbitrary"),
    )(tile_expert, tile_rows, n_used, xs, w_gate, b_gate.reshape(ne, 1, ff), w_up, b_up.reshape(ne, 1, ff))


def _moe_down_kernel(te_ref, tr_ref, nu_ref, hid_ref, wlo_ref, whi_ref, blo_ref, bhi_ref, ys_ref, *, sub):
    rows = tr_ref[pl.program_id(0)]
    tm = hid_ref.shape[0]
    n_sub = (rows + sub - 1) // sub

    for v in range(1, tm // sub + 1):
        m = v * sub

        @pl.when(n_sub == v)
        def _(m=m):
            ylo, yhi = _chunked_dots(hid_ref, m, (wlo_ref, whi_ref), MOE_K_CHUNK)
            ys_ref[0:m, :] = _pack_bf16_pair(ylo + blo_ref[...], yhi + bhi_ref[...])
            if m < tm:
                ys_ref[m:tm, :] = jnp.zeros((tm - m, ys_ref.shape[1]), I32)

    @pl.when(n_sub == 0)
    def _():
        ys_ref[...] = jnp.zeros_like(ys_ref)


def _moe_down(tile_expert, tile_rows, n_used, hid, w_down, b_down, tm, sub):
    rows, ff = hid.shape
    ne, d = w_down.shape[1], w_down.shape[3]
    half = d // 2
    tn = min(512, half)
    nj = half // tn
    row, col = _moe_index_helpers(nj)

    def w_spec(off):
        return pl.BlockSpec((None, None, ff, tn),
                            lambda i, j, te, tr, nu: (0, te[row(i, nu)], 0, col(i, j, nu) + off))

    def b_spec(off):
        return pl.BlockSpec((None, 1, tn), lambda i, j, te, tr, nu: (te[row(i, nu)], 0, col(i, j, nu) + off))

    return pl.pallas_call(
        functools.partial(_moe_down_kernel, sub=sub),
        grid_spec=pltpu.PrefetchScalarGridSpec(
            num_scalar_prefetch=3,
            grid=(rows // tm, nj),
            in_specs=[
                pl.BlockSpec((tm, ff), lambda i, j, te, tr, nu: (row(i, nu), 0)),
                w_spec(0), w_spec(nj), b_spec(0), b_spec(nj),
            ],
            out_specs=pl.BlockSpec((tm, tn), lambda i, j, te, tr, nu: (i, j)),
        ),
        out_shape=jax.ShapeDtypeStruct((rows, half), I32),
        compiler_params=_params("arbitrary", "arbitrary"),
    )(tile_expert, tile_rows, n_used, hid, w_down, w_down, b_down.reshape(ne, 1, d), b_down.reshape(ne, 1, d))


def _combine_kernel(pos_ref, ys_ref, x_ref, mg_ref, mod_ref, fg_ref, o_ref, ybuf, sem, *, tc, nt):
    base = (pl.program_id(0) * nt + pl.program_id(1)) * (tc * TOP_K)

    def copy(t, k):
        p = pos_ref[base + t * TOP_K + k]
        return pltpu.make_async_copy(ys_ref.at[pl.ds(p, 1)], ybuf.at[k, pl.ds(t, 1)], sem)

    def issue(t, carry):
        for k in range(TOP_K):
            copy(t, k).start()
        return carry

    def wait(t, carry):
        for k in range(TOP_K):
            copy(t, k).wait()
        return carry

    lax.fori_loop(0, tc, issue, 0)
    lax.fori_loop(0, tc, wait, 0)

    half = ybuf.shape[2]
    acc_lo = jnp.zeros((tc, half), F32)
    acc_hi = jnp.zeros((tc, half), F32)
    for k in range(TOP_K):
        lo, hi = _unpack_bf16_pair(ybuf[k])
        g = mg_ref[:, k:k + 1]
        acc_lo = acc_lo + lo * g
        acc_hi = acc_hi + hi * g
    x_lo = x_ref[:, :half] + mod_ref[5:6, :half] * acc_lo
    x_hi = x_ref[:, half:] + mod_ref[5:6, half:] * acc_hi
    ms = (jnp.sum(x_lo * x_lo, axis=-1, keepdims=True) + jnp.sum(x_hi * x_hi, axis=-1, keepdims=True)) / (2 * half)
    inv = lax.rsqrt(ms + EPS)
    o_ref[:, :half] = x_lo * inv * fg_ref[:, :half]
    o_ref[:, half:] = x_hi * inv * fg_ref[:, half:]


def _combine(pos_flat, ys, x1, mg, mod3, final_g):
    b, s, d = x1.shape
    tc = min(256, s)
    nt = s // tc
    kern = functools.partial(_combine_kernel, tc=tc, nt=nt)
    return pl.pallas_call(
        kern,
        grid_spec=pltpu.PrefetchScalarGridSpec(
            num_scalar_prefetch=1,
            grid=(b, nt),
            in_specs=[
                pl.BlockSpec(memory_space=pl.ANY),
                pl.BlockSpec((None, tc, d), lambda bi, i, pos: (bi, i, 0)),
                pl.BlockSpec((tc, LANES), lambda bi, i, pos: (bi * nt + i, 0)),
                pl.BlockSpec((None, N_MOD, d), lambda bi, i, pos: (bi, 0, 0)),
                pl.BlockSpec((1, d), lambda bi, i, pos: (0, 0)),
            ],
            out_specs=pl.BlockSpec((None, tc, d), lambda bi, i, pos: (bi, i, 0)),
            scratch_shapes=[pltpu.VMEM((TOP_K, tc, d // 2), I32), pltpu.SemaphoreType.DMA],
        ),
        out_shape=jax.ShapeDtypeStruct((b, s, d), F32),
        compiler_params=_params("arbitrary", "arbitrary"),
    )(pos_flat, ys, x1, mg, mod3, final_g)


def _rope_tables(seq_len):
    pairs = HEAD_DIM // 4
    rows = seq_len // GRID_W
    row = jnp.repeat(jnp.arange(rows, dtype=F32), GRID_W)
    col = jnp.tile(jnp.arange(GRID_W, dtype=F32), rows)
    freqs = ROPE_THETA ** (-jnp.arange(pairs, dtype=F32) / pairs)
    ang = jnp.concatenate([row[:, None] * freqs, col[:, None] * freqs], axis=-1)
    cos, sin = jnp.cos(ang), jnp.sin(ang)
    return jnp.concatenate([cos, cos], axis=-1), jnp.concatenate([-sin, sin], axis=-1)


def _routing_tables(counts, n_assign, tm, sub):
    ne = counts.shape[0]
    padded = (counts + tm - 1) // tm * tm
    pend = jnp.cumsum(padded)
    pstart = pend - padded
    n_tiles = n_assign // tm + ne
    tile_row0 = jnp.arange(n_tiles, dtype=I32) * tm
    tile_expert = jnp.minimum(jnp.sum((pend[None, :] <= tile_row0[:, None]).astype(I32), axis=1), ne - 1)
    tile_rows = jnp.clip(counts[tile_expert] - (tile_row0 - pstart[tile_expert]), 0, tm)
    tile_rows = jnp.where(tile_row0 < pend[-1], tile_rows, 0).astype(I32)
    n_used = (pend[-1:] // tm).astype(I32)
    zstart = (pstart + counts).astype(I32)
    zlen = ((counts + sub - 1) // sub * sub - counts).astype(I32)
    return pstart, n_tiles, tile_expert.astype(I32), tile_rows, n_used, zstart, zlen


def kernel(x, c, ctx, c_ctx, w_ada, b_ada, norm1_g, w_in, q_norm_g, k_norm_g, w_pool, pool_scale, w_out,
           norm2_g, w_router, b_router, w_gate, b_gate, w_up, b_up, w_down, b_down, final_g):
    b, s, d = x.shape
    assert w_ada.shape[0] == 1, "single-layer trunk"
    assert b + 1 <= MOD_ROWS
    pool_w = d // 4
    attn_w = d - pool_w
    n_kv = attn_w // HEAD_DIM // Q_PER_KV
    kv_w = n_kv * HEAD_DIM
    ne = w_router.shape[2]

    cc = jnp.concatenate([c, c_ctx[None, :], jnp.zeros((MOD_ROWS - b - 1, d), F32)], axis=0)
    mod3 = _adaln(cc, w_ada, b_ada).reshape(MOD_ROWS, N_MOD, d)

    cosf, sinf = _rope_tables(s)
    w_in = w_in.astype(BF16)
    w_out = w_out.astype(BF16)
    q_gain = q_norm_g * (ATTN_SCALE * LOG2_E)
    z = _inproj(x, mod3, None, norm1_g, w_in, 0, attn_w + 2 * kv_w + pool_w, attn_w, kv_w,
                q_gain, k_norm_g, cosf, sinf, True)
    zc = _inproj(ctx, mod3, b, norm1_g, w_in, attn_w, 2 * kv_w, 0, kv_w,
                 q_norm_g, k_norm_g, cosf, sinf, False)
    attn = _attention(z, zc, n_kv)
    pool = _pool(z, w_pool, pool_scale, attn_w + 2 * kv_w)
    x1 = _outproj(attn, pool, w_out, x, mod3)

    hp, mi, mg, counts = _router(x1, mod3, norm2_g, w_router, b_router)
    n = b * s
    tm = min(MOE_TILE_ROWS, n)
    sub = min(MOE_SUB_ROWS, tm)
    pstart, n_tiles, tile_expert, tile_rows, n_used, zstart, zlen = _routing_tables(
        counts[0].astype(I32), n * TOP_K, tm, sub)
    pos_flat = (pstart[mi[:, :TOP_K]] + mi[:, TOP_K:2 * TOP_K]).reshape(n * TOP_K)

    xs = _dispatch(pos_flat, zstart, zlen, hp, n_tiles * tm)
    hid = _moe_up(tile_expert, tile_rows, n_used, xs, w_gate, b_gate[0], w_up, b_up[0], tm, sub)
    ys = _moe_down(tile_expert, tile_rows, n_used, hid, w_down, b_down[0], tm, sub)
    return _combine(pos_flat, ys, x1, mg, mod3, final_g[None, :])
```

```python
import functools

import jax
import jax.numpy as jnp
from jax import lax
from jax.experimental import pallas as pl
from jax.experimental.pallas import tpu as pltpu

F32 = jnp.float32
BF16 = jnp.bfloat16
I32 = jnp.int32

EPS = 1e-6
HEAD_DIM = 128
Q_PER_KV = 3
GRID_W = 64
ROPE_THETA = 10000.0
POOL_WINDOWS = (2, 4, 8, 16)
TOP_K = 4
SWIGLU_ALPHA = 1.702
SWIGLU_LIMIT = 7.0
N_MOD = 6
ATTN_SCALE = HEAD_DIM ** -0.5
LOG2_E = 1.4426950408889634

V7X_VMEM_BYTES = 64 * 1024 * 1024
VMEM_LIMIT = V7X_VMEM_BYTES * 7 // 8
LANES = 128
SUBLANES = 8
ATTN_ROW_BLOCK = 128
NORM_ROW_CHUNK = 256
MOD_ROWS = 16
HI_MASK = -65536


def _params(*sem):
    return pltpu.CompilerParams(dimension_semantics=sem, vmem_limit_bytes=VMEM_LIMIT)


def _pack_bf16_pair(lo, hi):
    lo_bits = lax.bitcast_convert_type(lo.astype(BF16).astype(F32), I32)
    hi_bits = lax.bitcast_convert_type(hi.astype(BF16).astype(F32), I32)
    return lax.shift_right_logical(lo_bits, 16) | hi_bits


def _unpack_bf16_pair(p):
    lo = lax.bitcast_convert_type(lax.shift_left(p, 16), F32)
    hi = lax.bitcast_convert_type(p & HI_MASK, F32)
    return lo, hi


def _adaln_kernel(c_ref, w_ref, b_ref, o_ref):
    c = c_ref[...]
    a = (c * jax.nn.sigmoid(c)).astype(BF16)
    o_ref[...] = jnp.dot(a, w_ref[...].astype(BF16), preferred_element_type=F32) + b_ref[...]


def _adaln(cc, w_ada, b_ada):
    d = cc.shape[1]
    n = w_ada.shape[2]
    tn = min(512, n)
    return pl.pallas_call(
        _adaln_kernel,
        grid=(n // tn,),
        in_specs=[
            pl.BlockSpec((MOD_ROWS, d), lambda j: (0, 0)),
            pl.BlockSpec((None, d, tn), lambda j: (0, 0, j)),
            pl.BlockSpec((1, tn), lambda j: (0, j)),
        ],
        out_specs=pl.BlockSpec((MOD_ROWS, tn), lambda j: (0, j)),
        out_shape=jax.ShapeDtypeStruct((MOD_ROWS, n), F32),
        compiler_params=_params("arbitrary"),
    )(cc, w_ada, b_ada)


def _inproj_kernel(x_ref, mod_ref, g_ref, w_ref, qg_ref, kg_ref, cos_ref, sin_ref, o_ref, h_scr, z_scr,
                   *, nj, nq, nk, rope, heads_per_tile):
    j = pl.program_id(2)

    @pl.when(j == 0)
    def _():
        rc = min(NORM_ROW_CHUNK, x_ref.shape[0])

        def chunk(r, carry):
            rows = pl.ds(pl.multiple_of(r * rc, rc), rc)
            xf = x_ref[rows, :]
            ms = jnp.mean(xf * xf, axis=-1, keepdims=True)
            y = xf * lax.rsqrt(ms + EPS) * g_ref[...]
            h_scr[rows, :] = (y * (1 + mod_ref[1:2, :]) + mod_ref[0:1, :]).astype(BF16)
            return carry

        lax.fori_loop(0, x_ref.shape[0] // rc, chunk, 0)

        z_scr[...] = jnp.zeros_like(z_scr)

    def finish_previous_tile(z_prev):
        t = j - 1
        plain = t >= nq + nk
        g = jnp.where(t < nq, qg_ref[...], kg_ref[...])
        for hh in range(heads_per_tile):
            zc = z_prev[:, hh * HEAD_DIM:(hh + 1) * HEAD_DIM]
            ms = jnp.mean(zc * zc, axis=-1, keepdims=True)
            zn = zc * lax.rsqrt(ms + EPS) * g
            if rope:
                zn = zn * cos_ref[...] + pltpu.roll(zn, HEAD_DIM // 2, 1) * sin_ref[...]
            o_ref[:, hh * HEAD_DIM:(hh + 1) * HEAD_DIM] = jnp.where(plain, zc, zn).astype(BF16)

    for par in range(2):
        @pl.when((j < nj) & (j % 2 == par))
        def _(par=par):
            finish_previous_tile(z_scr.at[1 - par])
            z_scr[par] = jnp.dot(h_scr[...], w_ref[...], preferred_element_type=F32)

    @pl.when(j == nj)
    def _():
        finish_previous_tile(z_scr.at[(nj - 1) % 2])


def _inproj(x, mod3, mod_row, norm_g, w_in, col0, ncols, nq_cols, nk_cols, q_g, k_g, cosf, sinf, rope):
    b, l, d = x.shape
    tm = min(1024, l)
    tn = min(512, ncols)
    nj = ncols // tn
    kern = functools.partial(_inproj_kernel, nj=nj, nq=nq_cols // tn, nk=nk_cols // tn, rope=rope,
                             heads_per_tile=tn // HEAD_DIM)
    if mod_row is None:
        mod_map = lambda bi, i, j: (bi, 0, 0)
    else:
        mod_map = lambda bi, i, j: (mod_row, 0, 0)
    return pl.pallas_call(
        kern,
        grid=(b, l // tm, nj + 1),
        in_specs=[
            pl.BlockSpec((None, tm, d), lambda bi, i, j: (bi, i, 0), pipeline_mode=pl.Buffered(1)),
            pl.BlockSpec((None, N_MOD, d), mod_map),
            pl.BlockSpec((1, d), lambda bi, i, j: (0, 0)),
            pl.BlockSpec((None, d, tn), lambda bi, i, j: (0, 0, jnp.minimum(j, nj - 1) + col0 // tn)),
            pl.BlockSpec((1, HEAD_DIM), lambda bi, i, j: (0, 0)),
            pl.BlockSpec((1, HEAD_DIM), lambda bi, i, j: (0, 0)),
            pl.BlockSpec((tm, HEAD_DIM), lambda bi, i, j: (i, 0)),
            pl.BlockSpec((tm, HEAD_DIM), lambda bi, i, j: (i, 0)),
        ],
        out_specs=pl.BlockSpec((None, tm, tn), lambda bi, i, j: (bi, i, jnp.maximum(j - 1, 0))),
        out_shape=jax.ShapeDtypeStruct((b, l, ncols), BF16),
        scratch_shapes=[pltpu.VMEM((tm, d), BF16), pltpu.VMEM((2, tm, tn), F32)],
        compiler_params=_params("arbitrary", "arbitrary", "arbitrary"),
    )(x, mod3, norm_g, w_in, q_g, k_g, cosf, sinf)


def _attn_kernel(q_ref, kc_ref, vc_ref, k_ref, v_ref, o_ref):
    nt = (((1,), (1,)), ((), ()))
    vc = jnp.concatenate([vc_ref[...], jnp.ones(vc_ref.shape, BF16)], axis=1)
    vl = jnp.concatenate([v_ref[...], jnp.ones(v_ref.shape, BF16)], axis=1)
    tq = q_ref.shape[0]
    rb = min(ATTN_ROW_BLOCK, tq)
    blocks = [(r0, g * HEAD_DIM) for g in range(Q_PER_KV) for r0 in range(0, tq, rb)]

    def scores(blk):
        r0, c0 = blk
        q = q_ref[r0:r0 + rb, c0:c0 + HEAD_DIM]
        return (lax.dot_general(q, kc_ref[...], nt, preferred_element_type=F32),
                lax.dot_general(q, k_ref[...], nt, preferred_element_type=F32))

    nxt = scores(blocks[0])
    for bi, (r0, c0) in enumerate(blocks):
        sc, sl = nxt
        if bi + 1 < len(blocks):
            nxt = scores(blocks[bi + 1])
        m = jnp.maximum(jnp.max(sc, axis=-1, keepdims=True), jnp.max(sl, axis=-1, keepdims=True))
        pc = jnp.exp2((sc - m).astype(BF16))
        pk = jnp.exp2((sl - m).astype(BF16))
        o = jnp.dot(pc, vc, preferred_element_type=F32) + jnp.dot(pk, vl, preferred_element_type=F32)
        o_ref[r0:r0 + rb, c0:c0 + HEAD_DIM] = (o[:, :HEAD_DIM] / o[:, HEAD_DIM:HEAD_DIM + 1]).astype(BF16)


def _attention(z, zc, n_kv):
    b, s, _ = z.shape
    c = zc.shape[1]
    tq = min(512, s)
    gw = Q_PER_KV * HEAD_DIM
    n_q = n_kv * Q_PER_KV
    return pl.pallas_call(
        _attn_kernel,
        grid=(b, n_kv, s // tq),
        in_specs=[
            pl.BlockSpec((None, tq, gw), lambda bi, g, i: (bi, i, g)),
            pl.BlockSpec((None, c, HEAD_DIM), lambda bi, g, i: (bi, 0, g)),
            pl.BlockSpec((None, c, HEAD_DIM), lambda bi, g, i: (bi, 0, n_kv + g)),
            pl.BlockSpec((None, s, HEAD_DIM), lambda bi, g, i: (bi, 0, n_q + g)),
            pl.BlockSpec((None, s, HEAD_DIM), lambda bi, g, i: (bi, 0, n_q + n_kv + g)),
        ],
        out_specs=pl.BlockSpec((None, tq, gw), lambda bi, g, i: (bi, i, g)),
        out_shape=jax.ShapeDtypeStruct((b, s, n_q * HEAD_DIM), BF16),
        compiler_params=_params("arbitrary", "arbitrary", "arbitrary"),
    )(z, zc, zc, z, z)


def _pool_kernel(u_ref, w_ref, sc_ref, o_ref, d_scr):
    g = pl.program_id(1)
    n, gw = u_ref.shape
    t = lax.broadcasted_iota(I32, (n, LANES), 0)

    def shifted(a, d):
        r = pltpu.roll(a, (-d) % n, 0)
        ok = (t + d < n) if d > 0 else (t + d >= 0)
        return jnp.where(ok, r, 0.0)

    for gi, w in enumerate(POOL_WINDOWS):
        @pl.when(g == gi)
        def _(w=w):
            half = w // 2
            cnt = (jnp.minimum(t + half, n) - jnp.maximum(t - half, 0)).astype(F32)
            for c0 in range(0, gw, LANES):
                uf = u_ref[:, c0:c0 + LANES].astype(F32)
                fwd = uf
                bwd = shifted(uf, -1)
                span = 1
                while span < half:
                    fwd = fwd + shifted(fwd, span)
                    bwd = bwd + shifted(bwd, -span)
                    span *= 2
                d_scr[:, c0:c0 + LANES] = ((fwd + bwd) / cnt - uf).astype(BF16)

    y = jnp.dot(d_scr[...], w_ref[...].astype(BF16), preferred_element_type=F32)
    o_ref[...] = (y * sc_ref[...]).astype(BF16)


def _pool(z, w_pool, pool_scale, col0):
    b, s, _ = z.shape
    ng, gw = w_pool.shape[1], w_pool.shape[2]
    return pl.pallas_call(
        _pool_kernel,
        grid=(b, ng),
        in_specs=[
            pl.BlockSpec((None, s, gw), lambda bi, g: (bi, 0, col0 // gw + g)),
            pl.BlockSpec((None, None, gw, gw), lambda bi, g: (0, g, 0, 0)),
            pl.BlockSpec((None, 1, gw), lambda bi, g: (g, 0, 0)),
        ],
        out_specs=pl.BlockSpec((None, s, gw), lambda bi, g: (bi, 0, g)),
        out_shape=jax.ShapeDtypeStruct((b, s, ng * gw), BF16),
        scratch_shapes=[pltpu.VMEM((s, gw), BF16)],
        compiler_params=_params("arbitrary", "arbitrary"),
    )(z, w_pool, pool_scale.reshape(ng, 1, gw))


def _outproj_kernel(a_ref, p_ref, wa_ref, wp_ref, x_ref, mod_ref, o_ref):
    y = jnp.dot(a_ref[...], wa_ref[...], preferred_element_type=F32)
    y = y + jnp.dot(p_ref[...], wp_ref[...], preferred_element_type=F32)
    o_ref[...] = x_ref[...] + mod_ref[2:3, :] * y


def _outproj(attn, pool, w_out, x, mod3):
    b, s, d = x.shape
    aw, pw = attn.shape[2], pool.shape[2]
    tm = min(1024, s)
    tn = min(512, d)
    return pl.pallas_call(
        _outproj_kernel,
        grid=(b, s // tm, d // tn),
        in_specs=[
            pl.BlockSpec((None, tm, aw), lambda bi, i, j: (bi, i, 0)),
            pl.BlockSpec((None, tm, pw), lambda bi, i, j: (bi, i, 0)),
            pl.BlockSpec((None, aw, tn), lambda bi, i, j: (0, 0, j)),
            pl.BlockSpec((None, pw, tn), lambda bi, i, j: (0, aw // pw, j)),
            pl.BlockSpec((None, tm, tn), lambda bi, i, j: (bi, i, j)),
            pl.BlockSpec((None, N_MOD, tn), lambda bi, i, j: (bi, 0, j)),
        ],
        out_specs=pl.BlockSpec((None, tm, tn), lambda bi, i, j: (bi, i, j)),
        out_shape=jax.ShapeDtypeStruct((b, s, d), F32),
        compiler_params=_params("arbitrary", "arbitrary", "arbitrary"),
    )(attn, pool, w_out, w_out, x, mod3)


def _router_kernel(x_ref, mod_ref, g_ref, wr_ref, br_ref, hp_ref, mi_ref, mg_ref, cnt_ref, carry_scr,
                   *, tm, ne):
    first = (pl.program_id(0) == 0) & (pl.program_id(1) == 0)

    @pl.when(first)
    def _():
        carry_scr[...] = jnp.zeros_like(carry_scr)

    xf = x_ref[...]
    ms = jnp.mean(xf * xf, axis=-1, keepdims=True)
    y = xf * lax.rsqrt(ms + EPS) * g_ref[...]
    h = y * (1 + mod_ref[4:5, :]) + mod_ref[3:4, :]
    half = h.shape[1] // 2
    hp_ref[...] = _pack_bf16_pair(h[:, :half], h[:, half:])
    logits = jnp.dot(h.astype(BF16), wr_ref[...].astype(BF16), preferred_element_type=F32) + br_ref[...]

    lane = lax.broadcasted_iota(I32, logits.shape, 1).astype(F32)
    work = logits
    chosen = jnp.zeros(logits.shape, jnp.bool_)
    hits, idxs, vals = [], [], []
    for _ in range(TOP_K):
        m = jnp.max(work, axis=-1, keepdims=True)
        ik = jnp.min(jnp.where(work == m, lane, float(ne)), axis=-1, keepdims=True)
        hit = lane == ik
        hits.append(hit)
        idxs.append(ik.astype(I32))
        vals.append(m)
        chosen = chosen | hit
        work = jnp.where(hit, -jnp.inf, work)

    exps = [jnp.exp(v - vals[0]) for v in vals]
    den = exps[0]
    for e in exps[1:]:
        den = den + e
    gates = [e / den for e in exps]

    sel = chosen.astype(F32)
    r = lax.broadcasted_iota(I32, (tm, tm), 0)
    c = lax.broadcasted_iota(I32, (tm, tm), 1)
    tri = (c < r).astype(BF16)
    rank = carry_scr[...] + jnp.dot(tri, sel.astype(BF16), preferred_element_type=F32)
    carry_scr[...] = carry_scr[...] + jnp.sum(sel, axis=0, keepdims=True)
    cnt_ref[...] = carry_scr[...]
    ranks = [jnp.sum(jnp.where(hit, rank, 0.0), axis=-1, keepdims=True).astype(I32) for hit in hits]

    lane128 = lax.broadcasted_iota(I32, (tm, LANES), 1)
    mi = jnp.zeros((tm, LANES), I32)
    mg = jnp.zeros((tm, LANES), F32)
    for k in range(TOP_K):
        mi = jnp.where(lane128 == k, idxs[k], mi)
        mi = jnp.where(lane128 == TOP_K + k, ranks[k], mi)
        mg = jnp.where(lane128 == k, gates[k], mg)
    mi_ref[...] = mi
    mg_ref[...] = mg


def _router(x1, mod3, norm_g, w_router, b_router):
    b, s, d = x1.shape
    ne = w_router.shape[2]
    tm = min(256, s)
    nt = s // tm
    n = b * s
    kern = functools.partial(_router_kernel, tm=tm, ne=ne)
    return pl.pallas_call(
        kern,
        grid=(b, nt),
        in_specs=[
            pl.BlockSpec((None, tm, d), lambda bi, i: (bi, i, 0)),
            pl.BlockSpec((None, N_MOD, d), lambda bi, i: (bi, 0, 0)),
            pl.BlockSpec((1, d), lambda bi, i: (0, 0)),
            pl.BlockSpec((None, d, ne), lambda bi, i: (0, 0, 0)),
            pl.BlockSpec((1, ne), lambda bi, i: (0, 0)),
        ],
        out_specs=[
            pl.BlockSpec((tm, d // 2), lambda bi, i: (bi * nt + i, 0)),
            pl.BlockSpec((tm, LANES), lambda bi, i: (bi * nt + i, 0)),
            pl.BlockSpec((tm, LANES), lambda bi, i: (bi * nt + i, 0)),
            pl.BlockSpec((1, ne), lambda bi, i: (0, 0)),
        ],
        out_shape=[
            jax.ShapeDtypeStruct((n, d // 2), I32),
            jax.ShapeDtypeStruct((n, LANES), I32),
            jax.ShapeDtypeStruct((n, LANES), F32),
            jax.ShapeDtypeStruct((1, ne), F32),
        ],
        scratch_shapes=[pltpu.VMEM((1, ne), F32)],
        compiler_params=_params("arbitrary", "arbitrary"),
    )(x1, mod3, norm_g, w_router, b_router)


DISPATCH_WINDOW = 16
MOE_TILE_ROWS = 1024
MOE_SUB_ROWS = 256
MOE_K_CHUNK = 512


def _dispatch_kernel(pos_ref, zstart_ref, zlen_ref, hp_ref, xs_ref, zero_scr, sem, zsem, *, td, ne):
    bits = [1 << k for k in reversed(range(SUBLANES.bit_length() - 1, MOE_SUB_ROWS.bit_length() - 1))]

    def row_copy(e, r):
        return pltpu.make_async_copy(zero_scr.at[pl.ds(0, 1)], xs_ref.at[pl.ds(zstart_ref[e] + r, 1)], zsem)

    def chunk_copy(e, head, bit):
        off = pl.multiple_of(zstart_ref[e] + head + ((zlen_ref[e] - head) & ~(2 * bit - 1)), SUBLANES)
        return pltpu.make_async_copy(zero_scr.at[pl.ds(0, bit)], xs_ref.at[pl.ds(off, bit)], zsem)

    def zero_pass(start):
        def per_expert(e, carry):
            head = (-zstart_ref[e]) & (SUBLANES - 1)
            for r in range(SUBLANES - 1):
                @pl.when(r < head)
                def _(r=r):
                    row_copy(e, r).start() if start else row_copy(e, r).wait()
            for bit in bits:
                @pl.when(((zlen_ref[e] - head) & bit) != 0)
                def _(bit=bit):
                    chunk_copy(e, head, bit).start() if start else chunk_copy(e, head, bit).wait()
            return carry
        lax.fori_loop(0, ne, per_expert, 0)

    @pl.when(pl.program_id(0) == 0)
    def _():
        zero_scr[...] = jnp.zeros_like(zero_scr)
        zero_pass(True)
        zero_pass(False)

    base = pl.program_id(0) * (td * TOP_K)

    def copy(t, k):
        p = pos_ref[base + t * TOP_K + k]
        return pltpu.make_async_copy(hp_ref.at[pl.ds(t, 1)], xs_ref.at[pl.ds(p, 1)], sem)

    def body(t, carry):
        for k in range(TOP_K):
            copy(t, k).start()

        @pl.when(t >= DISPATCH_WINDOW)
        def _():
            for k in range(TOP_K):
                copy(t - DISPATCH_WINDOW, k).wait()

        return carry

    lax.fori_loop(0, td, body, 0)

    def drain(t, carry):
        for k in range(TOP_K):
            copy(t, k).wait()
        return carry

    lax.fori_loop(td - DISPATCH_WINDOW, td, drain, 0)


def _dispatch(pos_flat, zstart, zlen, hp, rows):
    n, w = hp.shape
    td = min(512, n)
    kern = functools.partial(_dispatch_kernel, td=td, ne=zstart.shape[0])
    return pl.pallas_call(
        kern,
        grid_spec=pltpu.PrefetchScalarGridSpec(
            num_scalar_prefetch=3,
            grid=(n // td,),
            in_specs=[pl.BlockSpec((td, w), lambda i, pos, zs, zl: (i, 0))],
            out_specs=pl.BlockSpec(memory_space=pl.ANY),
            scratch_shapes=[pltpu.VMEM((MOE_SUB_ROWS // 2, w), I32), pltpu.SemaphoreType.DMA,
                            pltpu.SemaphoreType.DMA],
        ),
        out_shape=jax.ShapeDtypeStruct((rows, w), I32),
        compiler_params=_params("arbitrary"),
    )(pos_flat, zstart, zlen, hp)


def _chunked_dots(x_ref, m, w_refs, kc):
    d = w_refs[0].shape[0]
    cast = lambda k0: [w[k0:k0 + kc, :].astype(BF16) for w in w_refs]
    accs = [None] * len(w_refs)
    nxt = cast(0)
    for k0 in range(0, d, kc):
        cur = nxt
        if k0 + kc < d:
            nxt = cast(k0 + kc)
        xk = x_ref[0:m, k0:k0 + kc]
        for n, wk in enumerate(cur):
            part = jnp.dot(xk, wk, preferred_element_type=F32)
            accs[n] = part if accs[n] is None else accs[n] + part
    return accs


def _moe_up_kernel(te_ref, tr_ref, nu_ref, xs_ref, wg_ref, bg_ref, wu_ref, bu_ref, hid_ref, xb_scr, *, sub):
    i = pl.program_id(0)
    j = pl.program_id(1)
    rows = tr_ref[i]
    tm, half = xs_ref.shape
    n_sub = (rows + sub - 1) // sub

    for v in range(1, tm // sub + 1):
        m = v * sub

        @pl.when(n_sub == v)
        def _(m=m):
            @pl.when(j == 0)
            def _():
                lo, hi = _unpack_bf16_pair(xs_ref[0:m, :])
                xb_scr[0:m, :half] = lo.astype(BF16)
                xb_scr[0:m, half:] = hi.astype(BF16)

            a, u = _chunked_dots(xb_scr, m, (wg_ref, wu_ref), MOE_K_CHUNK)
            a = jnp.minimum(a + bg_ref[...], SWIGLU_LIMIT)
            u = jnp.clip(u + bu_ref[...], -SWIGLU_LIMIT, SWIGLU_LIMIT)
            hid_ref[0:m, :] = ((u + 1) * (a * jax.nn.sigmoid(SWIGLU_ALPHA * a))).astype(BF16)
            if m < tm:
                hid_ref[m:tm, :] = jnp.zeros((tm - m, hid_ref.shape[1]), BF16)

    @pl.when(n_sub == 0)
    def _():
        hid_ref[...] = jnp.zeros_like(hid_ref)


def _moe_index_helpers(n_col_blocks):
    def row(i, nu):
        return jnp.minimum(i, nu[0] - 1)

    def col(i, j, nu):
        return jnp.where(i < nu[0], j, n_col_blocks - 1)

    return row, col


def _moe_up(tile_expert, tile_rows, n_used, xs, w_gate, b_gate, w_up, b_up, tm, sub):
    rows, half = xs.shape
    d = 2 * half
    ne, ff = w_gate.shape[1], w_gate.shape[3]
    tf = min(256, ff)
    nf = ff // tf
    row, col = _moe_index_helpers(nf)
    w_spec = pl.BlockSpec((None, None, d, tf), lambda i, j, te, tr, nu: (0, te[row(i, nu)], 0, col(i, j, nu)))
    b_spec = pl.BlockSpec((None, 1, tf), lambda i, j, te, tr, nu: (te[row(i, nu)], 0, col(i, j, nu)))
    return pl.pallas_call(
        functools.partial(_moe_up_kernel, sub=sub),
        grid_spec=pltpu.PrefetchScalarGridSpec(
            num_scalar_prefetch=3,
            grid=(rows // tm, nf),
            in_specs=[
                pl.BlockSpec((tm, half), lambda i, j, te, tr, nu: (row(i, nu), 0)),
                w_spec, b_spec, w_spec, b_spec,
            ],
            out_specs=pl.BlockSpec((tm, tf), lambda i, j, te, tr, nu: (i, j)),
            scratch_shapes=[pltpu.VMEM((tm, d), BF16)],
        ),
        out_shape=jax.ShapeDtypeStruct((rows, ff), BF16),
        compiler_params=_params("arbitrary", "arbitrary"),
    )(tile_expert, tile_rows, n_used, xs, w_gate, b_gate.reshape(ne, 1, ff), w_up, b_up.reshape(ne, 1, ff))


def _moe_down_kernel(te_ref, tr_ref, nu_ref, hid_ref, wlo_ref, whi_ref, blo_ref, bhi_ref, ys_ref, *, sub):
    rows = tr_ref[pl.program_id(0)]
    tm = hid_ref.shape[0]
    n_sub = (rows + sub - 1) // sub

    for v in range(1, tm // sub + 1):
        m = v * sub

        @pl.when(n_sub == v)
        def _(m=m):
            ylo, yhi = _chunked_dots(hid_ref, m, (wlo_ref, whi_ref), MOE_K_CHUNK)
            ys_ref[0:m, :] = _pack_bf16_pair(ylo + blo_ref[...], yhi + bhi_ref[...])
            if m < tm:
                ys_ref[m:tm, :] = jnp.zeros((tm - m, ys_ref.shape[1]), I32)

    @pl.when(n_sub == 0)
    def _():
        ys_ref[...] = jnp.zeros_like(ys_ref)


def _moe_down(tile_expert, tile_rows, n_used, hid, w_down, b_down, tm, sub):
    rows, ff = hid.shape
    ne, d = w_down.shape[1], w_down.shape[3]
    half = d // 2
    tn = min(512, half)
    nj = half // tn
    row, col = _moe_index_helpers(nj)

    def w_spec(off):
        return pl.BlockSpec((None, None, ff, tn),
                            lambda i, j, te, tr, nu: (0, te[row(i, nu)], 0, col(i, j, nu) + off))

    def b_spec(off):
        return pl.BlockSpec((None, 1, tn), lambda i, j, te, tr, nu: (te[row(i, nu)], 0, col(i, j, nu) + off))

    return pl.pallas_call(
        functools.partial(_moe_down_kernel, sub=sub),
        grid_spec=pltpu.PrefetchScalarGridSpec(
            num_scalar_prefetch=3,
            grid=(rows // tm, nj),
            in_specs=[
                pl.BlockSpec((tm, ff), lambda i, j, te, tr, nu: (row(i, nu), 0)),
                w_spec(0), w_spec(nj), b_spec(0), b_spec(nj),
            ],
            out_specs=pl.BlockSpec((tm, tn), lambda i, j, te, tr, nu: (i, j)),
        ),
        out_shape=jax.ShapeDtypeStruct((rows, half), I32),
        compiler_params=_params("arbitrary", "arbitrary"),
    )(tile_expert, tile_rows, n_used, hid, w_down, w_down, b_down.reshape(ne, 1, d), b_down.reshape(ne, 1, d))


def _combine_kernel(pos_ref, ys_ref, x_ref, mg_ref, mod_ref, fg_ref, o_ref, ybuf, sem, *, tc, nt):
    base = (pl.program_id(0) * nt + pl.program_id(1)) * (tc * TOP_K)

    def copy(t, k):
        p = pos_ref[base + t * TOP_K + k]
        return pltpu.make_async_copy(ys_ref.at[pl.ds(p, 1)], ybuf.at[k, pl.ds(t, 1)], sem)

    def issue(t, carry):
        for k in range(TOP_K):
            copy(t, k).start()
        return carry

    def wait(t, carry):
        for k in range(TOP_K):
            copy(t, k).wait()
        return carry

    lax.fori_loop(0, tc, issue, 0)
    lax.fori_loop(0, tc, wait, 0)

    half = ybuf.shape[2]
    acc_lo = jnp.zeros((tc, half), F32)
    acc_hi = jnp.zeros((tc, half), F32)
    for k in range(TOP_K):
        lo, hi = _unpack_bf16_pair(ybuf[k])
        g = mg_ref[:, k:k + 1]
        acc_lo = acc_lo + lo * g
        acc_hi = acc_hi + hi * g
    x_lo = x_ref[:, :half] + mod_ref[5:6, :half] * acc_lo
    x_hi = x_ref[:, half:] + mod_ref[5:6, half:] * acc_hi
    ms = (jnp.sum(x_lo * x_lo, axis=-1, keepdims=True) + jnp.sum(x_hi * x_hi, axis=-1, keepdims=True)) / (2 * half)
    inv = lax.rsqrt(ms + EPS)
    o_ref[:, :half] = x_lo * inv * fg_ref[:, :half]
    o_ref[:, half:] = x_hi * inv * fg_ref[:, half:]


def _combine(pos_flat, ys, x1, mg, mod3, final_g):
    b, s, d = x1.shape
    tc = min(256, s)
    nt = s // tc
    kern = functools.partial(_combine_kernel, tc=tc, nt=nt)
    return pl.pallas_call(
        kern,
        grid_spec=pltpu.PrefetchScalarGridSpec(
            num_scalar_prefetch=1,
            grid=(b, nt),
            in_specs=[
                pl.BlockSpec(memory_space=pl.ANY),
                pl.BlockSpec((None, tc, d), lambda bi, i, pos: (bi, i, 0)),
                pl.BlockSpec((tc, LANES), lambda bi, i, pos: (bi * nt + i, 0)),
                pl.BlockSpec((None, N_MOD, d), lambda bi, i, pos: (bi, 0, 0)),
                pl.BlockSpec((1, d), lambda bi, i, pos: (0, 0)),
            ],
            out_specs=pl.BlockSpec((None, tc, d), lambda bi, i, pos: (bi, i, 0)),
            scratch_shapes=[pltpu.VMEM((TOP_K, tc, d // 2), I32), pltpu.SemaphoreType.DMA],
        ),
        out_shape=jax.ShapeDtypeStruct((b, s, d), F32),
        compiler_params=_params("arbitrary", "arbitrary"),
    )(pos_flat, ys, x1, mg, mod3, final_g)


def _rope_tables(seq_len):
    pairs = HEAD_DIM // 4
    rows = seq_len // GRID_W
    row = jnp.repeat(jnp.arange(rows, dtype=F32), GRID_W)
    col = jnp.tile(jnp.arange(GRID_W, dtype=F32), rows)
    freqs = ROPE_THETA ** (-jnp.arange(pairs, dtype=F32) / pairs)
    ang = jnp.concatenate([row[:, None] * freqs, col[:, None] * freqs], axis=-1)
    cos, sin = jnp.cos(ang), jnp.sin(ang)
    return jnp.concatenate([cos, cos], axis=-1), jnp.concatenate([-sin, sin], axis=-1)


def _routing_tables(counts, n_assign, tm, sub):
    ne = counts.shape[0]
    nt = (counts + tm - 1) // tm
    rpt = jnp.maximum(((counts + jnp.maximum(nt, 1) - 1) // jnp.maximum(nt, 1) + sub - 1) // sub * sub, sub)
    pend = jnp.cumsum(nt * tm)
    pstart = pend - nt * tm
    n_tiles = n_assign // tm + ne
    tile_row0 = jnp.arange(n_tiles, dtype=I32) * tm
    tile_expert = jnp.minimum(jnp.sum((pend[None, :] <= tile_row0[:, None]).astype(I32), axis=1), ne - 1)
    tile_k = (tile_row0 - pstart[tile_expert]) // tm
    tile_rows = jnp.clip(counts[tile_expert] - tile_k * rpt[tile_expert], 0, rpt[tile_expert])
    tile_rows = jnp.where(tile_row0 < pend[-1], tile_rows, 0).astype(I32)
    n_used = (pend[-1:] // tm).astype(I32)
    last_rows = counts - (nt - 1) * rpt
    zstart = jnp.where(counts > 0, pstart + (nt - 1) * tm + last_rows, pstart).astype(I32)
    zlen = jnp.where(counts > 0, (last_rows + sub - 1) // sub * sub - last_rows, 0).astype(I32)
    return pstart, rpt, n_tiles, tile_expert.astype(I32), tile_rows, n_used, zstart, zlen


def kernel(x, c, ctx, c_ctx, w_ada, b_ada, norm1_g, w_in, q_norm_g, k_norm_g, w_pool, pool_scale, w_out,
           norm2_g, w_router, b_router, w_gate, b_gate, w_up, b_up, w_down, b_down, final_g):
    b, s, d = x.shape
    assert w_ada.shape[0] == 1, "single-layer trunk"
    assert b + 1 <= MOD_ROWS
    pool_w = d // 4
    attn_w = d - pool_w
    n_kv = attn_w // HEAD_DIM // Q_PER_KV
    kv_w = n_kv * HEAD_DIM
    ne = w_router.shape[2]

    cc = jnp.concatenate([c, c_ctx[None, :], jnp.zeros((MOD_ROWS - b - 1, d), F32)], axis=0)
    mod3 = _adaln(cc, w_ada, b_ada).reshape(MOD_ROWS, N_MOD, d)

    cosf, sinf = _rope_tables(s)
    w_in = w_in.astype(BF16)
    w_out = w_out.astype(BF16)
    q_gain = q_norm_g * (ATTN_SCALE * LOG2_E)
    z = _inproj(x, mod3, None, norm1_g, w_in, 0, attn_w + 2 * kv_w + pool_w, attn_w, kv_w,
                q_gain, k_norm_g, cosf, sinf, True)
    zc = _inproj(ctx, mod3, b, norm1_g, w_in, attn_w, 2 * kv_w, 0, kv_w,
                 q_norm_g, k_norm_g, cosf, sinf, False)
    attn = _attention(z, zc, n_kv)
    pool = _pool(z, w_pool, pool_scale, attn_w + 2 * kv_w)
    x1 = _outproj(attn, pool, w_out, x, mod3)

    hp, mi, mg, counts = _router(x1, mod3, norm2_g, w_router, b_router)
    n = b * s
    tm = min(MOE_TILE_ROWS, n)
    sub = min(MOE_SUB_ROWS, tm)
    pstart, rpt, n_tiles, tile_expert, tile_rows, n_used, zstart, zlen = _routing_tables(
        counts[0].astype(I32), n * TOP_K, tm, sub)
    idx, rank = mi[:, :TOP_K], mi[:, TOP_K:2 * TOP_K]
    tile_k = rank // rpt[idx]
    pos_flat = (pstart[idx] + tile_k * tm + (rank - tile_k * rpt[idx])).reshape(n * TOP_K)

    xs = _dispatch(pos_flat, zstart, zlen, hp, n_tiles * tm)
    hid = _moe_up(tile_expert, tile_rows, n_used, xs, w_gate, b_gate[0], w_up, b_up[0], tm, sub)
    ys = _moe_down(tile_expert, tile_rows, n_used, hid, w_down, b_down[0], tm, sub)
    return _combine(pos_flat, ys, x1, mg, mod3, final_g[None, :])
```

```python
import functools

import jax
import jax.numpy as jnp
from jax import lax
from jax.experimental import pallas as pl
from jax.experimental.pallas import tpu as pltpu

F32 = jnp.float32
BF16 = jnp.bfloat16
I32 = jnp.int32

EPS = 1e-6
HEAD_DIM = 128
Q_PER_KV = 3
GRID_W = 64
ROPE_THETA = 10000.0
POOL_WINDOWS = (2, 4, 8, 16)
TOP_K = 4
SWIGLU_ALPHA = 1.702
SWIGLU_LIMIT = 7.0
N_MOD = 6
ATTN_SCALE = HEAD_DIM ** -0.5
LOG2_E = 1.4426950408889634

V7X_VMEM_BYTES = 64 * 1024 * 1024
VMEM_LIMIT = V7X_VMEM_BYTES * 7 // 8
LANES = 128
SUBLANES = 8
ATTN_ROW_BLOCK = 128
NORM_ROW_CHUNK = 256
MOD_ROWS = 16
HI_MASK = -65536


def _params(*sem):
    return pltpu.CompilerParams(dimension_semantics=sem, vmem_limit_bytes=VMEM_LIMIT)


def _pack_bf16_pair(lo, hi):
    lo_bits = lax.bitcast_convert_type(lo.astype(BF16).astype(F32), I32)
    hi_bits = lax.bitcast_convert_type(hi.astype(BF16).astype(F32), I32)
    return lax.shift_right_logical(lo_bits, 16) | hi_bits


def _unpack_bf16_pair(p):
    lo = lax.bitcast_convert_type(lax.shift_left(p, 16), F32)
    hi = lax.bitcast_convert_type(p & HI_MASK, F32)
    return lo, hi


def _adaln_kernel(c_ref, w_ref, b_ref, o_ref):
    c = c_ref[...]
    a = (c * jax.nn.sigmoid(c)).astype(BF16)
    o_ref[...] = jnp.dot(a, w_ref[...].astype(BF16), preferred_element_type=F32) + b_ref[...]


def _adaln(cc, w_ada, b_ada):
    d = cc.shape[1]
    n = w_ada.shape[2]
    tn = min(512, n)
    return pl.pallas_call(
        _adaln_kernel,
        grid=(n // tn,),
        in_specs=[
            pl.BlockSpec((MOD_ROWS, d), lambda j: (0, 0)),
            pl.BlockSpec((None, d, tn), lambda j: (0, 0, j)),
            pl.BlockSpec((1, tn), lambda j: (0, j)),
        ],
        out_specs=pl.BlockSpec((MOD_ROWS, tn), lambda j: (0, j)),
        out_shape=jax.ShapeDtypeStruct((MOD_ROWS, n), F32),
        compiler_params=_params("arbitrary"),
    )(cc, w_ada, b_ada)


def _inproj_kernel(x_ref, mod_ref, g_ref, w_ref, qg_ref, kg_ref, cos_ref, sin_ref, o_ref, h_scr, z_scr,
                   *, nj, nq, nk, rope, heads_per_tile):
    j = pl.program_id(2)

    @pl.when(j == 0)
    def _():
        rc = min(NORM_ROW_CHUNK, x_ref.shape[0])

        def chunk(r, carry):
            rows = pl.ds(pl.multiple_of(r * rc, rc), rc)
            xf = x_ref[rows, :]
            ms = jnp.mean(xf * xf, axis=-1, keepdims=True)
            y = xf * lax.rsqrt(ms + EPS) * g_ref[...]
            h_scr[rows, :] = (y * (1 + mod_ref[1:2, :]) + mod_ref[0:1, :]).astype(BF16)
            return carry

        lax.fori_loop(0, x_ref.shape[0] // rc, chunk, 0)

        z_scr[...] = jnp.zeros_like(z_scr)

    def finish_previous_tile(z_prev):
        t = j - 1
        plain = t >= nq + nk
        g = jnp.where(t < nq, qg_ref[...], kg_ref[...])
        for hh in range(heads_per_tile):
            zc = z_prev[:, hh * HEAD_DIM:(hh + 1) * HEAD_DIM]
            ms = jnp.mean(zc * zc, axis=-1, keepdims=True)
            zn = zc * lax.rsqrt(ms + EPS) * g
            if rope:
                zn = zn * cos_ref[...] + pltpu.roll(zn, HEAD_DIM // 2, 1) * sin_ref[...]
            o_ref[:, hh * HEAD_DIM:(hh + 1) * HEAD_DIM] = jnp.where(plain, zc, zn).astype(BF16)

    for par in range(2):
        @pl.when((j < nj) & (j % 2 == par))
        def _(par=par):
            finish_previous_tile(z_scr.at[1 - par])
            z_scr[par] = jnp.dot(h_scr[...], w_ref[...], preferred_element_type=F32)

    @pl.when(j == nj)
    def _():
        finish_previous_tile(z_scr.at[(nj - 1) % 2])


def _inproj(x, mod3, mod_row, norm_g, w_in, col0, ncols, nq_cols, nk_cols, q_g, k_g, cosf, sinf, rope):
    b, l, d = x.shape
    tm = min(1024, l)
    tn = min(512, ncols)
    nj = ncols // tn
    kern = functools.partial(_inproj_kernel, nj=nj, nq=nq_cols // tn, nk=nk_cols // tn, rope=rope,
                             heads_per_tile=tn // HEAD_DIM)
    if mod_row is None:
        mod_map = lambda bi, i, j: (bi, 0, 0)
    else:
        mod_map = lambda bi, i, j: (mod_row, 0, 0)
    return pl.pallas_call(
        kern,
        grid=(b, l // tm, nj + 1),
        in_specs=[
            pl.BlockSpec((None, tm, d), lambda bi, i, j: (bi, i, 0), pipeline_mode=pl.Buffered(1)),
            pl.BlockSpec((None, N_MOD, d), mod_map),
            pl.BlockSpec((1, d), lambda bi, i, j: (0, 0)),
            pl.BlockSpec((None, d, tn), lambda bi, i, j: (0, 0, jnp.minimum(j, nj - 1) + col0 // tn)),
            pl.BlockSpec((1, HEAD_DIM), lambda bi, i, j: (0, 0)),
            pl.BlockSpec((1, HEAD_DIM), lambda bi, i, j: (0, 0)),
            pl.BlockSpec((tm, HEAD_DIM), lambda bi, i, j: (i, 0)),
            pl.BlockSpec((tm, HEAD_DIM), lambda bi, i, j: (i, 0)),
        ],
        out_specs=pl.BlockSpec((None, tm, tn), lambda bi, i, j: (bi, i, jnp.maximum(j - 1, 0))),
        out_shape=jax.ShapeDtypeStruct((b, l, ncols), BF16),
        scratch_shapes=[pltpu.VMEM((tm, d), BF16), pltpu.VMEM((2, tm, tn), F32)],
        compiler_params=_params("arbitrary", "arbitrary", "arbitrary"),
    )(x, mod3, norm_g, w_in, q_g, k_g, cosf, sinf)


def _attn_kernel(q_ref, kc_ref, vc_ref, k_ref, v_ref, o_ref):
    nt = (((1,), (1,)), ((), ()))
    vc = jnp.concatenate([vc_ref[...], jnp.ones(vc_ref.shape, BF16)], axis=1)
    vl = jnp.concatenate([v_ref[...], jnp.ones(v_ref.shape, BF16)], axis=1)
    tq = q_ref.shape[0]
    rb = min(ATTN_ROW_BLOCK, tq)
    blocks = [(r0, g * HEAD_DIM) for g in range(Q_PER_KV) for r0 in range(0, tq, rb)]

    def scores(blk):
        r0, c0 = blk
        q = q_ref[r0:r0 + rb, c0:c0 + HEAD_DIM]
        return (lax.dot_general(q, kc_ref[...], nt, preferred_element_type=F32),
                lax.dot_general(q, k_ref[...], nt, preferred_element_type=F32))

    nxt = scores(blocks[0])
    for bi, (r0, c0) in enumerate(blocks):
        sc, sl = nxt
        if bi + 1 < len(blocks):
            nxt = scores(blocks[bi + 1])
        m = jnp.maximum(jnp.max(sc, axis=-1, keepdims=True), jnp.max(sl, axis=-1, keepdims=True))
        pc = jnp.exp2((sc - m).astype(BF16))
        pk = jnp.exp2((sl - m).astype(BF16))
        o = jnp.dot(pc, vc, preferred_element_type=F32) + jnp.dot(pk, vl, preferred_element_type=F32)
        o_ref[r0:r0 + rb, c0:c0 + HEAD_DIM] = (o[:, :HEAD_DIM] / o[:, HEAD_DIM:HEAD_DIM + 1]).astype(BF16)


def _attention(z, zc, n_kv):
    b, s, _ = z.shape
    c = zc.shape[1]
    tq = min(512, s)
    gw = Q_PER_KV * HEAD_DIM
    n_q = n_kv * Q_PER_KV
    return pl.pallas_call(
        _attn_kernel,
        grid=(b, n_kv, s // tq),
        in_specs=[
            pl.BlockSpec((None, tq, gw), lambda bi, g, i: (bi, i, g)),
            pl.BlockSpec((None, c, HEAD_DIM), lambda bi, g, i: (bi, 0, g)),
            pl.BlockSpec((None, c, HEAD_DIM), lambda bi, g, i: (bi, 0, n_kv + g)),
            pl.BlockSpec((None, s, HEAD_DIM), lambda bi, g, i: (bi, 0, n_q + g)),
            pl.BlockSpec((None, s, HEAD_DIM), lambda bi, g, i: (bi, 0, n_q + n_kv + g)),
        ],
        out_specs=pl.BlockSpec((None, tq, gw), lambda bi, g, i: (bi, i, g)),
        out_shape=jax.ShapeDtypeStruct((b, s, n_q * HEAD_DIM), BF16),
        compiler_params=_params("arbitrary", "arbitrary", "arbitrary"),
    )(z, zc, zc, z, z)


def _pool_kernel(u_ref, w_ref, sc_ref, o_ref, d_scr):
    g = pl.program_id(1)
    n, gw = u_ref.shape
    t = lax.broadcasted_iota(I32, (n, LANES), 0)

    def shifted(a, d):
        r = pltpu.roll(a, (-d) % n, 0)
        ok = (t + d < n) if d > 0 else (t + d >= 0)
        return jnp.where(ok, r, 0.0)

    for gi, w in enumerate(POOL_WINDOWS):
        @pl.when(g == gi)
        def _(w=w):
            half = w // 2
            cnt = (jnp.minimum(t + half, n) - jnp.maximum(t - half, 0)).astype(F32)
            for c0 in range(0, gw, LANES):
                uf = u_ref[:, c0:c0 + LANES].astype(F32)
                fwd = uf
                bwd = shifted(uf, -1)
                span = 1
                while span < half:
                    fwd = fwd + shifted(fwd, span)
                    bwd = bwd + shifted(bwd, -span)
                    span *= 2
                d_scr[:, c0:c0 + LANES] = ((fwd + bwd) / cnt - uf).astype(BF16)

    y = jnp.dot(d_scr[...], w_ref[...].astype(BF16), preferred_element_type=F32)
    o_ref[...] = (y * sc_ref[...]).astype(BF16)


def _pool(z, w_pool, pool_scale, col0):
    b, s, _ = z.shape
    ng, gw = w_pool.shape[1], w_pool.shape[2]
    return pl.pallas_call(
        _pool_kernel,
        grid=(b, ng),
        in_specs=[
            pl.BlockSpec((None, s, gw), lambda bi, g: (bi, 0, col0 // gw + g)),
            pl.BlockSpec((None, None, gw, gw), lambda bi, g: (0, g, 0, 0)),
            pl.BlockSpec((None, 1, gw), lambda bi, g: (g, 0, 0)),
        ],
        out_specs=pl.BlockSpec((None, s, gw), lambda bi, g: (bi, 0, g)),
        out_shape=jax.ShapeDtypeStruct((b, s, ng * gw), BF16),
        scratch_shapes=[pltpu.VMEM((s, gw), BF16)],
        compiler_params=_params("arbitrary", "arbitrary"),
    )(z, w_pool, pool_scale.reshape(ng, 1, gw))


def _outproj_kernel(a_ref, p_ref, wa_ref, wp_ref, x_ref, mod_ref, o_ref):
    y = jnp.dot(a_ref[...], wa_ref[...], preferred_element_type=F32)
    y = y + jnp.dot(p_ref[...], wp_ref[...], preferred_element_type=F32)
    o_ref[...] = x_ref[...] + mod_ref[2:3, :] * y


def _outproj(attn, pool, w_out, x, mod3):
    b, s, d = x.shape
    aw, pw = attn.shape[2], pool.shape[2]
    tm = min(1024, s)
    tn = min(512, d)
    return pl.pallas_call(
        _outproj_kernel,
        grid=(b, s // tm, d // tn),
        in_specs=[
            pl.BlockSpec((None, tm, aw), lambda bi, i, j: (bi, i, 0)),
            pl.BlockSpec((None, tm, pw), lambda bi, i, j: (bi, i, 0)),
            pl.BlockSpec((None, aw, tn), lambda bi, i, j: (0, 0, j)),
            pl.BlockSpec((None, pw, tn), lambda bi, i, j: (0, aw // pw, j)),
            pl.BlockSpec((None, tm, tn), lambda bi, i, j: (bi, i, j)),
            pl.BlockSpec((None, N_MOD, tn), lambda bi, i, j: (bi, 0, j)),
        ],
        out_specs=pl.BlockSpec((None, tm, tn), lambda bi, i, j: (bi, i, j)),
        out_shape=jax.ShapeDtypeStruct((b, s, d), F32),
        compiler_params=_params("arbitrary", "arbitrary", "arbitrary"),
    )(attn, pool, w_out, w_out, x, mod3)


def _router_kernel(x_ref, mod_ref, g_ref, wr_ref, br_ref, hp_ref, mi_ref, mg_ref, cnt_ref, carry_scr,
                   *, tm, ne):
    first = (pl.program_id(0) == 0) & (pl.program_id(1) == 0)

    @pl.when(first)
    def _():
        carry_scr[...] = jnp.zeros_like(carry_scr)

    xf = x_ref[...]
    ms = jnp.mean(xf * xf, axis=-1, keepdims=True)
    y = xf * lax.rsqrt(ms + EPS) * g_ref[...]
    h = y * (1 + mod_ref[4:5, :]) + mod_ref[3:4, :]
    half = h.shape[1] // 2
    packed = _pack_bf16_pair(h[:, :half], h[:, half:])
    slab = half // LANES
    for c in range(slab):
        hp_ref[pl.ds(c, tm, stride=slab), :] = packed[:, c * LANES:(c + 1) * LANES]
    logits = jnp.dot(h.astype(BF16), wr_ref[...].astype(BF16), preferred_element_type=F32) + br_ref[...]

    lane = lax.broadcasted_iota(I32, logits.shape, 1).astype(F32)
    work = logits
    chosen = jnp.zeros(logits.shape, jnp.bool_)
    hits, idxs, vals = [], [], []
    for _ in range(TOP_K):
        m = jnp.max(work, axis=-1, keepdims=True)
        ik = jnp.min(jnp.where(work == m, lane, float(ne)), axis=-1, keepdims=True)
        hit = lane == ik
        hits.append(hit)
        idxs.append(ik.astype(I32))
        vals.append(m)
        chosen = chosen | hit
        work = jnp.where(hit, -jnp.inf, work)

    exps = [jnp.exp(v - vals[0]) for v in vals]
    den = exps[0]
    for e in exps[1:]:
        den = den + e
    gates = [e / den for e in exps]

    sel = chosen.astype(F32)
    r = lax.broadcasted_iota(I32, (tm, tm), 0)
    c = lax.broadcasted_iota(I32, (tm, tm), 1)
    tri = (c < r).astype(BF16)
    rank = carry_scr[...] + jnp.dot(tri, sel.astype(BF16), preferred_element_type=F32)
    carry_scr[...] = carry_scr[...] + jnp.sum(sel, axis=0, keepdims=True)
    cnt_ref[...] = carry_scr[...]
    ranks = [jnp.sum(jnp.where(hit, rank, 0.0), axis=-1, keepdims=True).astype(I32) for hit in hits]

    lane128 = lax.broadcasted_iota(I32, (tm, LANES), 1)
    mi = jnp.zeros((tm, LANES), I32)
    mg = jnp.zeros((tm, LANES), F32)
    for k in range(TOP_K):
        mi = jnp.where(lane128 == k, idxs[k], mi)
        mi = jnp.where(lane128 == TOP_K + k, ranks[k], mi)
        mg = jnp.where(lane128 == k, gates[k], mg)
    mi_ref[...] = mi
    mg_ref[...] = mg


def _router(x1, mod3, norm_g, w_router, b_router):
    b, s, d = x1.shape
    ne = w_router.shape[2]
    tm = min(256, s)
    nt = s // tm
    n = b * s
    kern = functools.partial(_router_kernel, tm=tm, ne=ne)
    return pl.pallas_call(
        kern,
        grid=(b, nt),
        in_specs=[
            pl.BlockSpec((None, tm, d), lambda bi, i: (bi, i, 0)),
            pl.BlockSpec((None, N_MOD, d), lambda bi, i: (bi, 0, 0)),
            pl.BlockSpec((1, d), lambda bi, i: (0, 0)),
            pl.BlockSpec((None, d, ne), lambda bi, i: (0, 0, 0)),
            pl.BlockSpec((1, ne), lambda bi, i: (0, 0)),
        ],
        out_specs=[
            pl.BlockSpec((tm * (d // 2 // LANES), LANES), lambda bi, i: (bi * nt + i, 0)),
            pl.BlockSpec((tm, LANES), lambda bi, i: (bi * nt + i, 0)),
            pl.BlockSpec((tm, LANES), lambda bi, i: (bi * nt + i, 0)),
            pl.BlockSpec((1, ne), lambda bi, i: (0, 0)),
        ],
        out_shape=[
            jax.ShapeDtypeStruct((n * (d // 2 // LANES), LANES), I32),
            jax.ShapeDtypeStruct((n, LANES), I32),
            jax.ShapeDtypeStruct((n, LANES), F32),
            jax.ShapeDtypeStruct((1, ne), F32),
        ],
        scratch_shapes=[pltpu.VMEM((1, ne), F32)],
        compiler_params=_params("arbitrary", "arbitrary"),
    )(x1, mod3, norm_g, w_router, b_router)


DISPATCH_WINDOW = 16
MOE_TILE_ROWS = 1024
MOE_SUB_ROWS = 256
MOE_K_CHUNK = 512


def _dispatch_kernel(pos_ref, zstart_ref, zlen_ref, hp_ref, xs_ref, zero_scr, sem, zsem, *, td, ne, slab):
    bits = [1 << k for k in reversed(range(MOE_SUB_ROWS.bit_length() - 1))]

    def rows_of(first_token, n_tokens):
        return pl.ds(pl.multiple_of(first_token * slab, slab), n_tokens * slab)

    def zero_copy(e, bit):
        first = zstart_ref[e] + (zlen_ref[e] & ~(2 * bit - 1))
        return pltpu.make_async_copy(zero_scr.at[pl.ds(0, bit * slab)], xs_ref.at[rows_of(first, bit)], zsem)

    def zero_pass(start):
        def per_expert(e, carry):
            for bit in bits:
                @pl.when((zlen_ref[e] & bit) != 0)
                def _(bit=bit):
                    zero_copy(e, bit).start() if start else zero_copy(e, bit).wait()
            return carry
        lax.fori_loop(0, ne, per_expert, 0)

    @pl.when(pl.program_id(0) == 0)
    def _():
        zero_scr[...] = jnp.zeros_like(zero_scr)
        zero_pass(True)
        zero_pass(False)

    base = pl.program_id(0) * (td * TOP_K)

    def copy(t, k):
        p = pos_ref[base + t * TOP_K + k]
        return pltpu.make_async_copy(hp_ref.at[rows_of(t, 1)], xs_ref.at[rows_of(p, 1)], sem)

    def body(t, carry):
        for k in range(TOP_K):
            copy(t, k).start()

        @pl.when(t >= DISPATCH_WINDOW)
        def _():
            for k in range(TOP_K):
                copy(t - DISPATCH_WINDOW, k).wait()

        return carry

    lax.fori_loop(0, td, body, 0)

    def drain(t, carry):
        for k in range(TOP_K):
            copy(t, k).wait()
        return carry

    lax.fori_loop(td - DISPATCH_WINDOW, td, drain, 0)


def _dispatch(pos_flat, zstart, zlen, hp, n, rows):
    slab = hp.shape[0] // n
    td = min(512, n)
    kern = functools.partial(_dispatch_kernel, td=td, ne=zstart.shape[0], slab=slab)
    return pl.pallas_call(
        kern,
        grid_spec=pltpu.PrefetchScalarGridSpec(
            num_scalar_prefetch=3,
            grid=(n // td,),
            in_specs=[pl.BlockSpec((td * slab, LANES), lambda i, pos, zs, zl: (i, 0))],
            out_specs=pl.BlockSpec(memory_space=pl.ANY),
            scratch_shapes=[pltpu.VMEM((MOE_SUB_ROWS // 2 * slab, LANES), I32), pltpu.SemaphoreType.DMA,
                            pltpu.SemaphoreType.DMA],
        ),
        out_shape=jax.ShapeDtypeStruct((rows * slab, LANES), I32),
        compiler_params=_params("arbitrary"),
    )(pos_flat, zstart, zlen, hp)


def _chunked_dots(x_ref, m, w_refs, kc):
    d = w_refs[0].shape[0]
    cast = lambda k0: [w[k0:k0 + kc, :].astype(BF16) for w in w_refs]
    accs = [None] * len(w_refs)
    nxt = cast(0)
    for k0 in range(0, d, kc):
        cur = nxt
        if k0 + kc < d:
            nxt = cast(k0 + kc)
        xk = x_ref[0:m, k0:k0 + kc]
        for n, wk in enumerate(cur):
            part = jnp.dot(xk, wk, preferred_element_type=F32)
            accs[n] = part if accs[n] is None else accs[n] + part
    return accs


def _moe_up_kernel(te_ref, tr_ref, nu_ref, xs_ref, wg_ref, bg_ref, wu_ref, bu_ref, hid_ref, xb_scr, *, sub):
    i = pl.program_id(0)
    j = pl.program_id(1)
    rows = tr_ref[i]
    tm, d = xb_scr.shape
    half = d // 2
    slab = half // LANES
    n_sub = (rows + sub - 1) // sub

    for v in range(1, tm // sub + 1):
        m = v * sub

        @pl.when(n_sub == v)
        def _(m=m):
            @pl.when(j == 0)
            def _():
                for c in range(slab):
                    lo, hi = _unpack_bf16_pair(xs_ref[pl.ds(c, m, stride=slab), :])
                    xb_scr[0:m, c * LANES:(c + 1) * LANES] = lo.astype(BF16)
                    xb_scr[0:m, half + c * LANES:half + (c + 1) * LANES] = hi.astype(BF16)

            a, u = _chunked_dots(xb_scr, m, (wg_ref, wu_ref), MOE_K_CHUNK)
            a = jnp.minimum(a + bg_ref[...], SWIGLU_LIMIT)
            u = jnp.clip(u + bu_ref[...], -SWIGLU_LIMIT, SWIGLU_LIMIT)
            hid_ref[0:m, :] = ((u + 1) * (a * jax.nn.sigmoid(SWIGLU_ALPHA * a))).astype(BF16)
            if m < tm:
                hid_ref[m:tm, :] = jnp.zeros((tm - m, hid_ref.shape[1]), BF16)

    @pl.when(n_sub == 0)
    def _():
        hid_ref[...] = jnp.zeros_like(hid_ref)


def _moe_index_helpers(n_col_blocks):
    def row(i, nu):
        return jnp.minimum(i, nu[0] - 1)

    def col(i, j, nu):
        return jnp.where(i < nu[0], j, n_col_blocks - 1)

    return row, col


def _moe_up(tile_expert, tile_rows, n_used, xs, w_gate, b_gate, w_up, b_up, tm, sub):
    ne, d, ff = w_gate.shape[1], w_gate.shape[2], w_gate.shape[3]
    slab = d // 2 // LANES
    rows = xs.shape[0] // slab
    tf = min(256, ff)
    nf = ff // tf
    row, col = _moe_index_helpers(nf)
    w_spec = pl.BlockSpec((None, None, d, tf), lambda i, j, te, tr, nu: (0, te[row(i, nu)], 0, col(i, j, nu)))
    b_spec = pl.BlockSpec((None, 1, tf), lambda i, j, te, tr, nu: (te[row(i, nu)], 0, col(i, j, nu)))
    return pl.pallas_call(
        functools.partial(_moe_up_kernel, sub=sub),
        grid_spec=pltpu.PrefetchScalarGridSpec(
            num_scalar_prefetch=3,
            grid=(rows // tm, nf),
            in_specs=[
                pl.BlockSpec((tm * slab, LANES), lambda i, j, te, tr, nu: (row(i, nu), 0)),
                w_spec, b_spec, w_spec, b_spec,
            ],
            out_specs=pl.BlockSpec((tm, tf), lambda i, j, te, tr, nu: (i, j)),
            scratch_shapes=[pltpu.VMEM((tm, d), BF16)],
        ),
        out_shape=jax.ShapeDtypeStruct((rows, ff), BF16),
        compiler_params=_params("arbitrary", "arbitrary"),
    )(tile_expert, tile_rows, n_used, xs, w_gate, b_gate.reshape(ne, 1, ff), w_up, b_up.reshape(ne, 1, ff))


def _moe_down_kernel(te_ref, tr_ref, nu_ref, hid_ref, wlo_ref, whi_ref, blo_ref, bhi_ref, ys_ref, *, sub):
    rows = tr_ref[pl.program_id(0)]
    tm = hid_ref.shape[0]
    n_sub = (rows + sub - 1) // sub

    for v in range(1, tm // sub + 1):
        m = v * sub

        @pl.when(n_sub == v)
        def _(m=m):
            ylo, yhi = _chunked_dots(hid_ref, m, (wlo_ref, whi_ref), MOE_K_CHUNK)
            ys_ref[0:m, :] = _pack_bf16_pair(ylo + blo_ref[...], yhi + bhi_ref[...])
            if m < tm:
                ys_ref[m:tm, :] = jnp.zeros((tm - m, ys_ref.shape[1]), I32)

    @pl.when(n_sub == 0)
    def _():
        ys_ref[...] = jnp.zeros_like(ys_ref)


def _moe_down(tile_expert, tile_rows, n_used, hid, w_down, b_down, tm, sub):
    rows, ff = hid.shape
    ne, d = w_down.shape[1], w_down.shape[3]
    half = d // 2
    tn = min(512, half)
    nj = half // tn
    row, col = _moe_index_helpers(nj)

    def w_spec(off):
        return pl.BlockSpec((None, None, ff, tn),
                            lambda i, j, te, tr, nu: (0, te[row(i, nu)], 0, col(i, j, nu) + off))

    def b_spec(off):
        return pl.BlockSpec((None, 1, tn), lambda i, j, te, tr, nu: (te[row(i, nu)], 0, col(i, j, nu) + off))

    return pl.pallas_call(
        functools.partial(_moe_down_kernel, sub=sub),
        grid_spec=pltpu.PrefetchScalarGridSpec(
            num_scalar_prefetch=3,
            grid=(rows // tm, nj),
            in_specs=[
                pl.BlockSpec((tm, ff), lambda i, j, te, tr, nu: (row(i, nu), 0)),
                w_spec(0), w_spec(nj), b_spec(0), b_spec(nj),
            ],
            out_specs=pl.BlockSpec((tm, tn), lambda i, j, te, tr, nu: (i, j)),
        ),
        out_shape=jax.ShapeDtypeStruct((rows, half), I32),
        compiler_params=_params("arbitrary", "arbitrary"),
    )(tile_expert, tile_rows, n_used, hid, w_down, w_down, b_down.reshape(ne, 1, d), b_down.reshape(ne, 1, d))


def _combine_kernel(pos_ref, ys_ref, x_ref, mg_ref, mod_ref, fg_ref, o_ref, ybuf, sem, *, tc, nt):
    base = (pl.program_id(0) * nt + pl.program_id(1)) * (tc * TOP_K)

    def copy(t, k):
        p = pos_ref[base + t * TOP_K + k]
        return pltpu.make_async_copy(ys_ref.at[pl.ds(p, 1)], ybuf.at[k, pl.ds(t, 1)], sem)

    def issue(t, carry):
        for k in range(TOP_K):
            copy(t, k).start()
        return carry

    def wait(t, carry):
        for k in range(TOP_K):
            copy(t, k).wait()
        return carry

    lax.fori_loop(0, tc, issue, 0)
    lax.fori_loop(0, tc, wait, 0)

    half = ybuf.shape[2]
    acc_lo = jnp.zeros((tc, half), F32)
    acc_hi = jnp.zeros((tc, half), F32)
    for k in range(TOP_K):
        lo, hi = _unpack_bf16_pair(ybuf[k])
        g = mg_ref[:, k:k + 1]
        acc_lo = acc_lo + lo * g
        acc_hi = acc_hi + hi * g
    x_lo = x_ref[:, :half] + mod_ref[5:6, :half] * acc_lo
    x_hi = x_ref[:, half:] + mod_ref[5:6, half:] * acc_hi
    ms = (jnp.sum(x_lo * x_lo, axis=-1, keepdims=True) + jnp.sum(x_hi * x_hi, axis=-1, keepdims=True)) / (2 * half)
    inv = lax.rsqrt(ms + EPS)
    o_ref[:, :half] = x_lo * inv * fg_ref[:, :half]
    o_ref[:, half:] = x_hi * inv * fg_ref[:, half:]


def _combine(pos_flat, ys, x1, mg, mod3, final_g):
    b, s, d = x1.shape
    tc = min(256, s)
    nt = s // tc
    kern = functools.partial(_combine_kernel, tc=tc, nt=nt)
    return pl.pallas_call(
        kern,
        grid_spec=pltpu.PrefetchScalarGridSpec(
            num_scalar_prefetch=1,
            grid=(b, nt),
            in_specs=[
                pl.BlockSpec(memory_space=pl.ANY),
                pl.BlockSpec((None, tc, d), lambda bi, i, pos: (bi, i, 0)),
                pl.BlockSpec((tc, LANES), lambda bi, i, pos: (bi * nt + i, 0)),
                pl.BlockSpec((None, N_MOD, d), lambda bi, i, pos: (bi, 0, 0)),
                pl.BlockSpec((1, d), lambda bi, i, pos: (0, 0)),
            ],
            out_specs=pl.BlockSpec((None, tc, d), lambda bi, i, pos: (bi, i, 0)),
            scratch_shapes=[pltpu.VMEM((TOP_K, tc, d // 2), I32), pltpu.SemaphoreType.DMA],
        ),
        out_shape=jax.ShapeDtypeStruct((b, s, d), F32),
        compiler_params=_params("arbitrary", "arbitrary"),
    )(pos_flat, ys, x1, mg, mod3, final_g)


def _rope_tables(seq_len):
    pairs = HEAD_DIM // 4
    rows = seq_len // GRID_W
    row = jnp.repeat(jnp.arange(rows, dtype=F32), GRID_W)
    col = jnp.tile(jnp.arange(GRID_W, dtype=F32), rows)
    freqs = ROPE_THETA ** (-jnp.arange(pairs, dtype=F32) / pairs)
    ang = jnp.concatenate([row[:, None] * freqs, col[:, None] * freqs], axis=-1)
    cos, sin = jnp.cos(ang), jnp.sin(ang)
    return jnp.concatenate([cos, cos], axis=-1), jnp.concatenate([-sin, sin], axis=-1)


def _routing_tables(counts, n_assign, tm, sub):
    ne = counts.shape[0]
    nt = (counts + tm - 1) // tm
    rpt = jnp.maximum(((counts + jnp.maximum(nt, 1) - 1) // jnp.maximum(nt, 1) + sub - 1) // sub * sub, sub)
    pend = jnp.cumsum(nt * tm)
    pstart = pend - nt * tm
    n_tiles = n_assign // tm + ne
    tile_row0 = jnp.arange(n_tiles, dtype=I32) * tm
    tile_expert = jnp.minimum(jnp.sum((pend[None, :] <= tile_row0[:, None]).astype(I32), axis=1), ne - 1)
    tile_k = (tile_row0 - pstart[tile_expert]) // tm
    tile_rows = jnp.clip(counts[tile_expert] - tile_k * rpt[tile_expert], 0, rpt[tile_expert])
    tile_rows = jnp.where(tile_row0 < pend[-1], tile_rows, 0).astype(I32)
    n_used = (pend[-1:] // tm).astype(I32)
    last_rows = counts - (nt - 1) * rpt
    zstart = jnp.where(counts > 0, pstart + (nt - 1) * tm + last_rows, pstart).astype(I32)
    zlen = jnp.where(counts > 0, (last_rows + sub - 1) // sub * sub - last_rows, 0).astype(I32)
    return pstart, rpt, n_tiles, tile_expert.astype(I32), tile_rows, n_used, zstart, zlen


def kernel(x, c, ctx, c_ctx, w_ada, b_ada, norm1_g, w_in, q_norm_g, k_norm_g, w_pool, pool_scale, w_out,
           norm2_g, w_router, b_router, w_gate, b_gate, w_up, b_up, w_down, b_down, final_g):
    b, s, d = x.shape
    assert w_ada.shape[0] == 1, "single-layer trunk"
    assert b + 1 <= MOD_ROWS
    pool_w = d // 4
    attn_w = d - pool_w
    n_kv = attn_w // HEAD_DIM // Q_PER_KV
    kv_w = n_kv * HEAD_DIM
    ne = w_router.shape[2]

    cc = jnp.concatenate([c, c_ctx[None, :], jnp.zeros((MOD_ROWS - b - 1, d), F32)], axis=0)
    mod3 = _adaln(cc, w_ada, b_ada).reshape(MOD_ROWS, N_MOD, d)

    cosf, sinf = _rope_tables(s)
    w_in = w_in.astype(BF16)
    w_out = w_out.astype(BF16)
    q_gain = q_norm_g * (ATTN_SCALE * LOG2_E)
    z = _inproj(x, mod3, None, norm1_g, w_in, 0, attn_w + 2 * kv_w + pool_w, attn_w, kv_w,
                q_gain, k_norm_g, cosf, sinf, True)
    zc = _inproj(ctx, mod3, b, norm1_g, w_in, attn_w, 2 * kv_w, 0, kv_w,
                 q_norm_g, k_norm_g, cosf, sinf, False)
    attn = _attention(z, zc, n_kv)
    pool = _pool(z, w_pool, pool_scale, attn_w + 2 * kv_w)
    x1 = _outproj(attn, pool, w_out, x, mod3)

    hp, mi, mg, counts = _router(x1, mod3, norm2_g, w_router, b_router)
    n = b * s
    tm = min(MOE_TILE_ROWS, n)
    sub = min(MOE_SUB_ROWS, tm)
    pstart, rpt, n_tiles, tile_expert, tile_rows, n_used, zstart, zlen = _routing_tables(
        counts[0].astype(I32), n * TOP_K, tm, sub)
    idx, rank = mi[:, :TOP_K], mi[:, TOP_K:2 * TOP_K]
    tile_k = jnp.floor((rank.astype(F32) + 0.5) / rpt[idx].astype(F32)).astype(I32)
    pos_flat = (pstart[idx] + tile_k * tm + (rank - tile_k * rpt[idx])).reshape(n * TOP_K)

    xs = _dispatch(pos_flat, zstart, zlen, hp, n, n_tiles * tm)
    hid = _moe_up(tile_expert, tile_rows, n_used, xs, w_gate, b_gate[0], w_up, b_up[0], tm, sub)
    ys = _moe_down(tile_expert, tile_rows, n_used, hid, w_down, b_down[0], tm, sub)
    return _combine(pos_flat, ys, x1, mg, mod3, final_g[None, :])
```

```python
import functools

import jax
import jax.numpy as jnp
from jax import lax
from jax.experimental import pallas as pl
from jax.experimental.pallas import tpu as pltpu

F32 = jnp.float32
BF16 = jnp.bfloat16
I32 = jnp.int32

EPS = 1e-6
HEAD_DIM = 128
Q_PER_KV = 3
GRID_W = 64
ROPE_THETA = 10000.0
POOL_WINDOWS = (2, 4, 8, 16)
TOP_K = 4
SWIGLU_ALPHA = 1.702
SWIGLU_LIMIT = 7.0
N_MOD = 6
ATTN_SCALE = HEAD_DIM ** -0.5
LOG2_E = 1.4426950408889634

V7X_VMEM_BYTES = 64 * 1024 * 1024
VMEM_LIMIT = V7X_VMEM_BYTES * 7 // 8
LANES = 128
SUBLANES = 8
ATTN_ROW_BLOCK = 128
NORM_ROW_CHUNK = 256
MOD_ROWS = 16
HI_MASK = -65536


def _params(*sem):
    return pltpu.CompilerParams(dimension_semantics=sem, vmem_limit_bytes=VMEM_LIMIT)


def _pack_bf16_pair(lo, hi):
    lo_bits = lax.bitcast_convert_type(lo.astype(BF16).astype(F32), I32)
    hi_bits = lax.bitcast_convert_type(hi.astype(BF16).astype(F32), I32)
    return lax.shift_right_logical(lo_bits, 16) | hi_bits


def _unpack_bf16_pair(p):
    lo = lax.bitcast_convert_type(lax.shift_left(p, 16), F32)
    hi = lax.bitcast_convert_type(p & HI_MASK, F32)
    return lo, hi


def _adaln_kernel(c_ref, w_ref, b_ref, o_ref):
    c = c_ref[...]
    a = (c * jax.nn.sigmoid(c)).astype(BF16)
    o_ref[...] = jnp.dot(a, w_ref[...].astype(BF16), preferred_element_type=F32) + b_ref[...]


def _adaln(cc, w_ada, b_ada):
    d = cc.shape[1]
    n = w_ada.shape[2]
    tn = min(512, n)
    return pl.pallas_call(
        _adaln_kernel,
        grid=(n // tn,),
        in_specs=[
            pl.BlockSpec((MOD_ROWS, d), lambda j: (0, 0)),
            pl.BlockSpec((None, d, tn), lambda j: (0, 0, j)),
            pl.BlockSpec((1, tn), lambda j: (0, j)),
        ],
        out_specs=pl.BlockSpec((MOD_ROWS, tn), lambda j: (0, j)),
        out_shape=jax.ShapeDtypeStruct((MOD_ROWS, n), F32),
        compiler_params=_params("arbitrary"),
    )(cc, w_ada, b_ada)


def _inproj_kernel(x_ref, mod_ref, g_ref, w_ref, qg_ref, kg_ref, cos_ref, sin_ref, o_ref, h_scr, z_scr,
                   *, nj, nq, nk, rope, heads_per_tile):
    j = pl.program_id(2)

    @pl.when(j == 0)
    def _():
        rc = min(NORM_ROW_CHUNK, x_ref.shape[0])

        def chunk(r, carry):
            rows = pl.ds(pl.multiple_of(r * rc, rc), rc)
            xf = x_ref[rows, :]
            ms = jnp.mean(xf * xf, axis=-1, keepdims=True)
            y = xf * lax.rsqrt(ms + EPS) * g_ref[...]
            h_scr[rows, :] = (y * (1 + mod_ref[1:2, :]) + mod_ref[0:1, :]).astype(BF16)
            return carry

        lax.fori_loop(0, x_ref.shape[0] // rc, chunk, 0)

        z_scr[...] = jnp.zeros_like(z_scr)

    def finish_previous_tile(z_prev):
        t = j - 1
        plain = t >= nq + nk
        g = jnp.where(t < nq, qg_ref[...], kg_ref[...])
        for hh in range(heads_per_tile):
            zc = z_prev[:, hh * HEAD_DIM:(hh + 1) * HEAD_DIM]
            ms = jnp.mean(zc * zc, axis=-1, keepdims=True)
            zn = zc * lax.rsqrt(ms + EPS) * g
            if rope:
                zn = zn * cos_ref[...] + pltpu.roll(zn, HEAD_DIM // 2, 1) * sin_ref[...]
            o_ref[:, hh * HEAD_DIM:(hh + 1) * HEAD_DIM] = jnp.where(plain, zc, zn).astype(BF16)

    for par in range(2):
        @pl.when((j < nj) & (j % 2 == par))
        def _(par=par):
            finish_previous_tile(z_scr.at[1 - par])
            z_scr[par] = jnp.dot(h_scr[...], w_ref[...], preferred_element_type=F32)

    @pl.when(j == nj)
    def _():
        finish_previous_tile(z_scr.at[(nj - 1) % 2])


def _inproj(x, mod3, mod_row, norm_g, w_in, col0, ncols, nq_cols, nk_cols, q_g, k_g, cosf, sinf, rope):
    b, l, d = x.shape
    tm = min(1024, l)
    tn = min(512, ncols)
    nj = ncols // tn
    kern = functools.partial(_inproj_kernel, nj=nj, nq=nq_cols // tn, nk=nk_cols // tn, rope=rope,
                             heads_per_tile=tn // HEAD_DIM)
    if mod_row is None:
        mod_map = lambda bi, i, j: (bi, 0, 0)
    else:
        mod_map = lambda bi, i, j: (mod_row, 0, 0)
    return pl.pallas_call(
        kern,
        grid=(b, l // tm, nj + 1),
        in_specs=[
            pl.BlockSpec((None, tm, d), lambda bi, i, j: (bi, i, 0), pipeline_mode=pl.Buffered(1)),
            pl.BlockSpec((None, N_MOD, d), mod_map),
            pl.BlockSpec((1, d), lambda bi, i, j: (0, 0)),
            pl.BlockSpec((None, d, tn), lambda bi, i, j: (0, 0, jnp.minimum(j, nj - 1) + col0 // tn)),
            pl.BlockSpec((1, HEAD_DIM), lambda bi, i, j: (0, 0)),
            pl.BlockSpec((1, HEAD_DIM), lambda bi, i, j: (0, 0)),
            pl.BlockSpec((tm, HEAD_DIM), lambda bi, i, j: (i, 0)),
            pl.BlockSpec((tm, HEAD_DIM), lambda bi, i, j: (i, 0)),
        ],
        out_specs=pl.BlockSpec((None, tm, tn), lambda bi, i, j: (bi, i, jnp.maximum(j - 1, 0))),
        out_shape=jax.ShapeDtypeStruct((b, l, ncols), BF16),
        scratch_shapes=[pltpu.VMEM((tm, d), BF16), pltpu.VMEM((2, tm, tn), F32)],
        compiler_params=_params("arbitrary", "arbitrary", "arbitrary"),
    )(x, mod3, norm_g, w_in, q_g, k_g, cosf, sinf)


def _attn_kernel(q_ref, kc_ref, vc_ref, k_ref, v_ref, o_ref):
    nt = (((1,), (1,)), ((), ()))
    vc = jnp.concatenate([vc_ref[...], jnp.ones(vc_ref.shape, BF16)], axis=1)
    vl = jnp.concatenate([v_ref[...], jnp.ones(v_ref.shape, BF16)], axis=1)
    tq = q_ref.shape[0]
    rb = min(ATTN_ROW_BLOCK, tq)
    blocks = [(r0, g * HEAD_DIM) for g in range(Q_PER_KV) for r0 in range(0, tq, rb)]

    def scores(blk):
        r0, c0 = blk
        q = q_ref[r0:r0 + rb, c0:c0 + HEAD_DIM]
        return (lax.dot_general(q, kc_ref[...], nt, preferred_element_type=F32),
                lax.dot_general(q, k_ref[...], nt, preferred_element_type=F32))

    nxt = scores(blocks[0])
    for bi, (r0, c0) in enumerate(blocks):
        sc, sl = nxt
        if bi + 1 < len(blocks):
            nxt = scores(blocks[bi + 1])
        m = jnp.maximum(jnp.max(sc, axis=-1, keepdims=True), jnp.max(sl, axis=-1, keepdims=True))
        pc = jnp.exp2((sc - m).astype(BF16))
        pk = jnp.exp2((sl - m).astype(BF16))
        o = jnp.dot(pc, vc, preferred_element_type=F32) + jnp.dot(pk, vl, preferred_element_type=F32)
        o_ref[r0:r0 + rb, c0:c0 + HEAD_DIM] = (o[:, :HEAD_DIM] / o[:, HEAD_DIM:HEAD_DIM + 1]).astype(BF16)


def _attention(z, zc, n_kv):
    b, s, _ = z.shape
    c = zc.shape[1]
    tq = min(512, s)
    gw = Q_PER_KV * HEAD_DIM
    n_q = n_kv * Q_PER_KV
    return pl.pallas_call(
        _attn_kernel,
        grid=(b, n_kv, s // tq),
        in_specs=[
            pl.BlockSpec((None, tq, gw), lambda bi, g, i: (bi, i, g)),
            pl.BlockSpec((None, c, HEAD_DIM), lambda bi, g, i: (bi, 0, g)),
            pl.BlockSpec((None, c, HEAD_DIM), lambda bi, g, i: (bi, 0, n_kv + g)),
            pl.BlockSpec((None, s, HEAD_DIM), lambda bi, g, i: (bi, 0, n_q + g)),
            pl.BlockSpec((None, s, HEAD_DIM), lambda bi, g, i: (bi, 0, n_q + n_kv + g)),
        ],
        out_specs=pl.BlockSpec((None, tq, gw), lambda bi, g, i: (bi, i, g)),
        out_shape=jax.ShapeDtypeStruct((b, s, n_q * HEAD_DIM), BF16),
        compiler_params=_params("arbitrary", "arbitrary", "arbitrary"),
    )(z, zc, zc, z, z)


def _pool_kernel(u_ref, w_ref, sc_ref, o_ref, d_scr):
    g = pl.program_id(1)
    n, gw = u_ref.shape
    t = lax.broadcasted_iota(I32, (n, LANES), 0)

    def shifted(a, d):
        r = pltpu.roll(a, (-d) % n, 0)
        ok = (t + d < n) if d > 0 else (t + d >= 0)
        return jnp.where(ok, r, 0.0)

    for gi, w in enumerate(POOL_WINDOWS):
        @pl.when(g == gi)
        def _(w=w):
            half = w // 2
            cnt = (jnp.minimum(t + half, n) - jnp.maximum(t - half, 0)).astype(F32)
            for c0 in range(0, gw, LANES):
                uf = u_ref[:, c0:c0 + LANES].astype(F32)
                fwd = uf
                bwd = shifted(uf, -1)
                span = 1
                while span < half:
                    fwd = fwd + shifted(fwd, span)
                    bwd = bwd + shifted(bwd, -span)
                    span *= 2
                d_scr[:, c0:c0 + LANES] = ((fwd + bwd) / cnt - uf).astype(BF16)

    y = jnp.dot(d_scr[...], w_ref[...].astype(BF16), preferred_element_type=F32)
    o_ref[...] = (y * sc_ref[...]).astype(BF16)


def _pool(z, w_pool, pool_scale, col0):
    b, s, _ = z.shape
    ng, gw = w_pool.shape[1], w_pool.shape[2]
    return pl.pallas_call(
        _pool_kernel,
        grid=(b, ng),
        in_specs=[
            pl.BlockSpec((None, s, gw), lambda bi, g: (bi, 0, col0 // gw + g)),
            pl.BlockSpec((None, None, gw, gw), lambda bi, g: (0, g, 0, 0)),
            pl.BlockSpec((None, 1, gw), lambda bi, g: (g, 0, 0)),
        ],
        out_specs=pl.BlockSpec((None, s, gw), lambda bi, g: (bi, 0, g)),
        out_shape=jax.ShapeDtypeStruct((b, s, ng * gw), BF16),
        scratch_shapes=[pltpu.VMEM((s, gw), BF16)],
        compiler_params=_params("arbitrary", "arbitrary"),
    )(z, w_pool, pool_scale.reshape(ng, 1, gw))


def _outproj_kernel(a_ref, p_ref, wa_ref, wp_ref, x_ref, mod_ref, o_ref):
    y = jnp.dot(a_ref[...], wa_ref[...], preferred_element_type=F32)
    y = y + jnp.dot(p_ref[...], wp_ref[...], preferred_element_type=F32)
    o_ref[...] = x_ref[...] + mod_ref[2:3, :] * y


def _outproj(attn, pool, w_out, x, mod3):
    b, s, d = x.shape
    aw, pw = attn.shape[2], pool.shape[2]
    tm = min(1024, s)
    tn = min(512, d)
    return pl.pallas_call(
        _outproj_kernel,
        grid=(b, s // tm, d // tn),
        in_specs=[
            pl.BlockSpec((None, tm, aw), lambda bi, i, j: (bi, i, 0)),
            pl.BlockSpec((None, tm, pw), lambda bi, i, j: (bi, i, 0)),
            pl.BlockSpec((None, aw, tn), lambda bi, i, j: (0, 0, j)),
            pl.BlockSpec((None, pw, tn), lambda bi, i, j: (0, aw // pw, j)),
            pl.BlockSpec((None, tm, tn), lambda bi, i, j: (bi, i, j)),
            pl.BlockSpec((None, N_MOD, tn), lambda bi, i, j: (bi, 0, j)),
        ],
        out_specs=pl.BlockSpec((None, tm, tn), lambda bi, i, j: (bi, i, j)),
        out_shape=jax.ShapeDtypeStruct((b, s, d), F32),
        compiler_params=_params("arbitrary", "arbitrary", "arbitrary"),
    )(attn, pool, w_out, w_out, x, mod3)


def _router_kernel(x_ref, mod_ref, g_ref, wr_ref, br_ref, hp_ref, mi_ref, mg_ref, cnt_ref, carry_scr,
                   *, tm, ne):
    first = (pl.program_id(0) == 0) & (pl.program_id(1) == 0)

    @pl.when(first)
    def _():
        carry_scr[...] = jnp.zeros_like(carry_scr)

    xf = x_ref[...]
    ms = jnp.mean(xf * xf, axis=-1, keepdims=True)
    y = xf * lax.rsqrt(ms + EPS) * g_ref[...]
    h = y * (1 + mod_ref[4:5, :]) + mod_ref[3:4, :]
    half = h.shape[1] // 2
    hp_ref[...] = _pack_bf16_pair(h[:, :half], h[:, half:])
    logits = jnp.dot(h.astype(BF16), wr_ref[...].astype(BF16), preferred_element_type=F32) + br_ref[...]

    lane = lax.broadcasted_iota(I32, logits.shape, 1).astype(F32)
    work = logits
    chosen = jnp.zeros(logits.shape, jnp.bool_)
    hits, idxs, vals = [], [], []
    for _ in range(TOP_K):
        m = jnp.max(work, axis=-1, keepdims=True)
        ik = jnp.min(jnp.where(work == m, lane, float(ne)), axis=-1, keepdims=True)
        hit = lane == ik
        hits.append(hit)
        idxs.append(ik.astype(I32))
        vals.append(m)
        chosen = chosen | hit
        work = jnp.where(hit, -jnp.inf, work)

    exps = [jnp.exp(v - vals[0]) for v in vals]
    den = exps[0]
    for e in exps[1:]:
        den = den + e
    gates = [e / den for e in exps]

    sel = chosen.astype(F32)
    r = lax.broadcasted_iota(I32, (tm, tm), 0)
    c = lax.broadcasted_iota(I32, (tm, tm), 1)
    tri = (c < r).astype(BF16)
    rank = carry_scr[...] + jnp.dot(tri, sel.astype(BF16), preferred_element_type=F32)
    carry_scr[...] = carry_scr[...] + jnp.sum(sel, axis=0, keepdims=True)
    cnt_ref[...] = carry_scr[...]
    ranks = [jnp.sum(jnp.where(hit, rank, 0.0), axis=-1, keepdims=True).astype(I32) for hit in hits]

    lane128 = lax.broadcasted_iota(I32, (tm, LANES), 1)
    mi = jnp.zeros((tm, LANES), I32)
    mg = jnp.zeros((tm, LANES), F32)
    for k in range(TOP_K):
        mi = jnp.where(lane128 == k, idxs[k], mi)
        mi = jnp.where(lane128 == TOP_K + k, ranks[k], mi)
        mg = jnp.where(lane128 == k, gates[k], mg)
    mi_ref[...] = mi
    mg_ref[...] = mg


def _router(x1, mod3, norm_g, w_router, b_router):
    b, s, d = x1.shape
    ne = w_router.shape[2]
    tm = min(256, s)
    nt = s // tm
    n = b * s
    kern = functools.partial(_router_kernel, tm=tm, ne=ne)
    return pl.pallas_call(
        kern,
        grid=(b, nt),
        in_specs=[
            pl.BlockSpec((None, tm, d), lambda bi, i: (bi, i, 0)),
            pl.BlockSpec((None, N_MOD, d), lambda bi, i: (bi, 0, 0)),
            pl.BlockSpec((1, d), lambda bi, i: (0, 0)),
            pl.BlockSpec((None, d, ne), lambda bi, i: (0, 0, 0)),
            pl.BlockSpec((1, ne), lambda bi, i: (0, 0)),
        ],
        out_specs=[
            pl.BlockSpec((tm, d // 2), lambda bi, i: (bi * nt + i, 0)),
            pl.BlockSpec((tm, LANES), lambda bi, i: (bi * nt + i, 0)),
            pl.BlockSpec((tm, LANES), lambda bi, i: (bi * nt + i, 0)),
            pl.BlockSpec((1, ne), lambda bi, i: (0, 0)),
        ],
        out_shape=[
            jax.ShapeDtypeStruct((n, d // 2), I32),
            jax.ShapeDtypeStruct((n, LANES), I32),
            jax.ShapeDtypeStruct((n, LANES), F32),
            jax.ShapeDtypeStruct((1, ne), F32),
        ],
        scratch_shapes=[pltpu.VMEM((1, ne), F32)],
        compiler_params=_params("arbitrary", "arbitrary"),
    )(x1, mod3, norm_g, w_router, b_router)


DISPATCH_WINDOW = 16
DMA_PRIORITIES = 2
MOE_TILE_ROWS = 1024
MOE_SUB_ROWS = 256
MOE_K_CHUNK = 512


def _dispatch_kernel(pos_ref, zstart_ref, zlen_ref, hp_ref, xs_ref, zero_scr, sem, zsem, *, td, ne):
    bits = [1 << k for k in reversed(range(SUBLANES.bit_length() - 1, MOE_SUB_ROWS.bit_length() - 1))]

    def row_copy(e, r):
        return pltpu.make_async_copy(zero_scr.at[pl.ds(0, 1)], xs_ref.at[pl.ds(zstart_ref[e] + r, 1)], zsem)

    def chunk_copy(e, head, bit):
        off = pl.multiple_of(zstart_ref[e] + head + ((zlen_ref[e] - head) & ~(2 * bit - 1)), SUBLANES)
        return pltpu.make_async_copy(zero_scr.at[pl.ds(0, bit)], xs_ref.at[pl.ds(off, bit)], zsem)

    def zero_pass(start):
        def per_expert(e, carry):
            head = (-zstart_ref[e]) & (SUBLANES - 1)
            for r in range(SUBLANES - 1):
                @pl.when(r < head)
                def _(r=r):
                    row_copy(e, r).start() if start else row_copy(e, r).wait()
            for bit in bits:
                @pl.when(((zlen_ref[e] - head) & bit) != 0)
                def _(bit=bit):
                    chunk_copy(e, head, bit).start() if start else chunk_copy(e, head, bit).wait()
            return carry
        lax.fori_loop(0, ne, per_expert, 0)

    @pl.when(pl.program_id(0) == 0)
    def _():
        zero_scr[...] = jnp.zeros_like(zero_scr)
        zero_pass(True)
        zero_pass(False)

    base = pl.program_id(0) * (td * TOP_K)

    def copy(t, k):
        p = pos_ref[base + t * TOP_K + k]
        return pltpu.make_async_copy(hp_ref.at[pl.ds(t, 1)], xs_ref.at[pl.ds(p, 1)], sem)

    def body(t, carry):
        for k in range(TOP_K):
            copy(t, k).start(priority=k % DMA_PRIORITIES)

        @pl.when(t >= DISPATCH_WINDOW)
        def _():
            for k in range(TOP_K):
                copy(t - DISPATCH_WINDOW, k).wait()

        return carry

    lax.fori_loop(0, td, body, 0)

    def drain(t, carry):
        for k in range(TOP_K):
            copy(t, k).wait()
        return carry

    lax.fori_loop(td - DISPATCH_WINDOW, td, drain, 0)


def _dispatch(pos_flat, zstart, zlen, hp, rows):
    n, w = hp.shape
    td = min(512, n)
    kern = functools.partial(_dispatch_kernel, td=td, ne=zstart.shape[0])
    return pl.pallas_call(
        kern,
        grid_spec=pltpu.PrefetchScalarGridSpec(
            num_scalar_prefetch=3,
            grid=(n // td,),
            in_specs=[pl.BlockSpec((td, w), lambda i, pos, zs, zl: (i, 0))],
            out_specs=pl.BlockSpec(memory_space=pl.ANY),
            scratch_shapes=[pltpu.VMEM((MOE_SUB_ROWS // 2, w), I32), pltpu.SemaphoreType.DMA,
                            pltpu.SemaphoreType.DMA],
        ),
        out_shape=jax.ShapeDtypeStruct((rows, w), I32),
        compiler_params=_params("arbitrary"),
    )(pos_flat, zstart, zlen, hp)


def _chunked_dots(x_ref, m, w_refs, kc):
    d = w_refs[0].shape[0]
    cast = lambda k0: [w[k0:k0 + kc, :].astype(BF16) for w in w_refs]
    accs = [None] * len(w_refs)
    nxt = cast(0)
    for k0 in range(0, d, kc):
        cur = nxt
        if k0 + kc < d:
            nxt = cast(k0 + kc)
        xk = x_ref[0:m, k0:k0 + kc]
        for n, wk in enumerate(cur):
            part = jnp.dot(xk, wk, preferred_element_type=F32)
            accs[n] = part if accs[n] is None else accs[n] + part
    return accs


def _moe_up_kernel(te_ref, tr_ref, nu_ref, xs_ref, wg_ref, bg_ref, wu_ref, bu_ref, hid_ref, xb_scr, *, sub):
    i = pl.program_id(0)
    j = pl.program_id(1)
    rows = tr_ref[i]
    tm, half = xs_ref.shape
    n_sub = (rows + sub - 1) // sub

    for v in range(1, tm // sub + 1):
        m = v * sub

        @pl.when(n_sub == v)
        def _(m=m):
            @pl.when(j == 0)
            def _():
                lo, hi = _unpack_bf16_pair(xs_ref[0:m, :])
                xb_scr[0:m, :half] = lo.astype(BF16)
                xb_scr[0:m, half:] = hi.astype(BF16)

            a, u = _chunked_dots(xb_scr, m, (wg_ref, wu_ref), MOE_K_CHUNK)
            a = jnp.minimum(a + bg_ref[...], SWIGLU_LIMIT)
            u = jnp.clip(u + bu_ref[...], -SWIGLU_LIMIT, SWIGLU_LIMIT)
            hid_ref[0:m, :] = ((u + 1) * (a * jax.nn.sigmoid(SWIGLU_ALPHA * a))).astype(BF16)
            if m < tm:
                hid_ref[m:tm, :] = jnp.zeros((tm - m, hid_ref.shape[1]), BF16)

    @pl.when(n_sub == 0)
    def _():
        hid_ref[...] = jnp.zeros_like(hid_ref)


def _moe_index_helpers(n_col_blocks):
    def row(i, nu):
        return jnp.minimum(i, nu[0] - 1)

    def col(i, j, nu):
        return jnp.where(i < nu[0], j, n_col_blocks - 1)

    return row, col


def _moe_up(tile_expert, tile_rows, n_used, xs, w_gate, b_gate, w_up, b_up, tm, sub):
    rows, half = xs.shape
    d = 2 * half
    ne, ff = w_gate.shape[1], w_gate.shape[3]
    tf = min(256, ff)
    nf = ff // tf
    row, col = _moe_index_helpers(nf)
    w_spec = pl.BlockSpec((None, None, d, tf), lambda i, j, te, tr, nu: (0, te[row(i, nu)], 0, col(i, j, nu)))
    b_spec = pl.BlockSpec((None, 1, tf), lambda i, j, te, tr, nu: (te[row(i, nu)], 0, col(i, j, nu)))
    return pl.pallas_call(
        functools.partial(_moe_up_kernel, sub=sub),
        grid_spec=pltpu.PrefetchScalarGridSpec(
            num_scalar_prefetch=3,
            grid=(rows // tm, nf),
            in_specs=[
                pl.BlockSpec((tm, half), lambda i, j, te, tr, nu: (row(i, nu), 0)),
                w_spec, b_spec, w_spec, b_spec,
            ],
            out_specs=pl.BlockSpec((tm, tf), lambda i, j, te, tr, nu: (i, j)),
            scratch_shapes=[pltpu.VMEM((tm, d), BF16)],
        ),
        out_shape=jax.ShapeDtypeStruct((rows, ff), BF16),
        compiler_params=_params("arbitrary", "arbitrary"),
    )(tile_expert, tile_rows, n_used, xs, w_gate, b_gate.reshape(ne, 1, ff), w_up, b_up.reshape(ne, 1, ff))


def _moe_down_kernel(te_ref, tr_ref, nu_ref, hid_ref, wlo_ref, whi_ref, blo_ref, bhi_ref, ys_ref, *, sub):
    rows = tr_ref[pl.program_id(0)]
    tm = hid_ref.shape[0]
    n_sub = (rows + sub - 1) // sub

    for v in range(1, tm // sub + 1):
        m = v * sub

        @pl.when(n_sub == v)
        def _(m=m):
            ylo, yhi = _chunked_dots(hid_ref, m, (wlo_ref, whi_ref), MOE_K_CHUNK)
            ys_ref[0:m, :] = _pack_bf16_pair(ylo + blo_ref[...], yhi + bhi_ref[...])
            if m < tm:
                ys_ref[m:tm, :] = jnp.zeros((tm - m, ys_ref.shape[1]), I32)

    @pl.when(n_sub == 0)
    def _():
        ys_ref[...] = jnp.zeros_like(ys_ref)


def _moe_down(tile_expert, tile_rows, n_used, hid, w_down, b_down, tm, sub):
    rows, ff = hid.shape
    ne, d = w_down.shape[1], w_down.shape[3]
    half = d // 2
    tn = min(512, half)
    nj = half // tn
    row, col = _moe_index_helpers(nj)

    def w_spec(off):
        return pl.BlockSpec((None, None, ff, tn),
                            lambda i, j, te, tr, nu: (0, te[row(i, nu)], 0, col(i, j, nu) + off))

    def b_spec(off):
        return pl.BlockSpec((None, 1, tn), lambda i, j, te, tr, nu: (te[row(i, nu)], 0, col(i, j, nu) + off))

    return pl.pallas_call(
        functools.partial(_moe_down_kernel, sub=sub),
        grid_spec=pltpu.PrefetchScalarGridSpec(
            num_scalar_prefetch=3,
            grid=(rows // tm, nj),
            in_specs=[
                pl.BlockSpec((tm, ff), lambda i, j, te, tr, nu: (row(i, nu), 0)),
                w_spec(0), w_spec(nj), b_spec(0), b_spec(nj),
            ],
            out_specs=pl.BlockSpec((tm, tn), lambda i, j, te, tr, nu: (i, j)),
        ),
        out_shape=jax.ShapeDtypeStruct((rows, half), I32),
        compiler_params=_params("arbitrary", "arbitrary"),
    )(tile_expert, tile_rows, n_used, hid, w_down, w_down, b_down.reshape(ne, 1, d), b_down.reshape(ne, 1, d))


def _combine_kernel(pos_ref, ys_ref, x_ref, mg_ref, mod_ref, fg_ref, o_ref, ybuf, sem, *, tc, nt):
    base = (pl.program_id(0) * nt + pl.program_id(1)) * (tc * TOP_K)

    def copy(t, k):
        p = pos_ref[base + t * TOP_K + k]
        return pltpu.make_async_copy(ys_ref.at[pl.ds(p, 1)], ybuf.at[k, pl.ds(t, 1)], sem)

    def issue(t, carry):
        for k in range(TOP_K):
            copy(t, k).start(priority=k % DMA_PRIORITIES)
        return carry

    def wait(t, carry):
        for k in range(TOP_K):
            copy(t, k).wait()
        return carry

    lax.fori_loop(0, tc, issue, 0)
    lax.fori_loop(0, tc, wait, 0)

    half = ybuf.shape[2]
    acc_lo = jnp.zeros((tc, half), F32)
    acc_hi = jnp.zeros((tc, half), F32)
    for k in range(TOP_K):
        lo, hi = _unpack_bf16_pair(ybuf[k])
        g = mg_ref[:, k:k + 1]
        acc_lo = acc_lo + lo * g
        acc_hi = acc_hi + hi * g
    x_lo = x_ref[:, :half] + mod_ref[5:6, :half] * acc_lo
    x_hi = x_ref[:, half:] + mod_ref[5:6, half:] * acc_hi
    ms = (jnp.sum(x_lo * x_lo, axis=-1, keepdims=True) + jnp.sum(x_hi * x_hi, axis=-1, keepdims=True)) / (2 * half)
    inv = lax.rsqrt(ms + EPS)
    o_ref[:, :half] = x_lo * inv * fg_ref[:, :half]
    o_ref[:, half:] = x_hi * inv * fg_ref[:, half:]


def _combine(pos_flat, ys, x1, mg, mod3, final_g):
    b, s, d = x1.shape
    tc = min(256, s)
    nt = s // tc
    kern = functools.partial(_combine_kernel, tc=tc, nt=nt)
    return pl.pallas_call(
        kern,
        grid_spec=pltpu.PrefetchScalarGridSpec(
            num_scalar_prefetch=1,
            grid=(b, nt),
            in_specs=[
                pl.BlockSpec(memory_space=pl.ANY),
                pl.BlockSpec((None, tc, d), lambda bi, i, pos: (bi, i, 0)),
                pl.BlockSpec((tc, LANES), lambda bi, i, pos: (bi * nt + i, 0)),
                pl.BlockSpec((None, N_MOD, d), lambda bi, i, pos: (bi, 0, 0)),
                pl.BlockSpec((1, d), lambda bi, i, pos: (0, 0)),
            ],
            out_specs=pl.BlockSpec((None, tc, d), lambda bi, i, pos: (bi, i, 0)),
            scratch_shapes=[pltpu.VMEM((TOP_K, tc, d // 2), I32), pltpu.SemaphoreType.DMA],
        ),
        out_shape=jax.ShapeDtypeStruct((b, s, d), F32),
        compiler_params=_params("arbitrary", "arbitrary"),
    )(pos_flat, ys, x1, mg, mod3, final_g)


def _rope_tables(seq_len):
    pairs = HEAD_DIM // 4
    rows = seq_len // GRID_W
    row = jnp.repeat(jnp.arange(rows, dtype=F32), GRID_W)
    col = jnp.tile(jnp.arange(GRID_W, dtype=F32), rows)
    freqs = ROPE_THETA ** (-jnp.arange(pairs, dtype=F32) / pairs)
    ang = jnp.concatenate([row[:, None] * freqs, col[:, None] * freqs], axis=-1)
    cos, sin = jnp.cos(ang), jnp.sin(ang)
    return jnp.concatenate([cos, cos], axis=-1), jnp.concatenate([-sin, sin], axis=-1)


def _routing_tables(counts, n_assign, tm, sub):
    ne = counts.shape[0]
    nt = (counts + tm - 1) // tm
    rpt = jnp.maximum(((counts + jnp.maximum(nt, 1) - 1) // jnp.maximum(nt, 1) + sub - 1) // sub * sub, sub)
    pend = jnp.cumsum(nt * tm)
    pstart = pend - nt * tm
    n_tiles = n_assign // tm + ne
    tile_row0 = jnp.arange(n_tiles, dtype=I32) * tm
    tile_expert = jnp.minimum(jnp.sum((pend[None, :] <= tile_row0[:, None]).astype(I32), axis=1), ne - 1)
    tile_k = (tile_row0 - pstart[tile_expert]) // tm
    tile_rows = jnp.clip(counts[tile_expert] - tile_k * rpt[tile_expert], 0, rpt[tile_expert])
    tile_rows = jnp.where(tile_row0 < pend[-1], tile_rows, 0).astype(I32)
    n_used = (pend[-1:] // tm).astype(I32)
    last_rows = counts - (nt - 1) * rpt
    zstart = jnp.where(counts > 0, pstart + (nt - 1) * tm + last_rows, pstart).astype(I32)
    zlen = jnp.where(counts > 0, (last_rows + sub - 1) // sub * sub - last_rows, 0).astype(I32)
    return pstart, rpt, n_tiles, tile_expert.astype(I32), tile_rows, n_used, zstart, zlen


def kernel(x, c, ctx, c_ctx, w_ada, b_ada, norm1_g, w_in, q_norm_g, k_norm_g, w_pool, pool_scale, w_out,
           norm2_g, w_router, b_router, w_gate, b_gate, w_up, b_up, w_down, b_down, final_g):
    b, s, d = x.shape
    assert w_ada.shape[0] == 1, "single-layer trunk"
    assert b + 1 <= MOD_ROWS
    pool_w = d // 4
    attn_w = d - pool_w
    n_kv = attn_w // HEAD_DIM // Q_PER_KV
    kv_w = n_kv * HEAD_DIM
    ne = w_router.shape[2]

    cc = jnp.concatenate([c, c_ctx[None, :], jnp.zeros((MOD_ROWS - b - 1, d), F32)], axis=0)
    mod3 = _adaln(cc, w_ada, b_ada).reshape(MOD_ROWS, N_MOD, d)

    cosf, sinf = _rope_tables(s)
    w_in = w_in.astype(BF16)
    w_out = w_out.astype(BF16)
    q_gain = q_norm_g * (ATTN_SCALE * LOG2_E)
    z = _inproj(x, mod3, None, norm1_g, w_in, 0, attn_w + 2 * kv_w + pool_w, attn_w, kv_w,
                q_gain, k_norm_g, cosf, sinf, True)
    zc = _inproj(ctx, mod3, b, norm1_g, w_in, attn_w, 2 * kv_w, 0, kv_w,
                 q_norm_g, k_norm_g, cosf, sinf, False)
    attn = _attention(z, zc, n_kv)
    pool = _pool(z, w_pool, pool_scale, attn_w + 2 * kv_w)
    x1 = _outproj(attn, pool, w_out, x, mod3)

    hp, mi, mg, counts = _router(x1, mod3, norm2_g, w_router, b_router)
    n = b * s
    tm = min(MOE_TILE_ROWS, n)
    sub = min(MOE_SUB_ROWS, tm)
    pstart, rpt, n_tiles, tile_expert, tile_rows, n_used, zstart, zlen = _routing_tables(
        counts[0].astype(I32), n * TOP_K, tm, sub)
    idx, rank = mi[:, :TOP_K], mi[:, TOP_K:2 * TOP_K]
    tile_k = jnp.floor((rank.astype(F32) + 0.5) / rpt[idx].astype(F32)).astype(I32)
    pos_flat = (pstart[idx] + tile_k * tm + (rank - tile_k * rpt[idx])).reshape(n * TOP_K)

    xs = _dispatch(pos_flat, zstart, zlen, hp, n_tiles * tm)
    hid = _moe_up(tile_expert, tile_rows, n_used, xs, w_gate, b_gate[0], w_up, b_up[0], tm, sub)
    ys = _moe_down(tile_expert, tile_rows, n_used, hid, w_down, b_down[0], tm, sub)
    return _combine(pos_flat, ys, x1, mg, mod3, final_g[None, :])
```

```python
import functools

import jax
import jax.numpy as jnp
from jax import lax
from jax.experimental import pallas as pl
from jax.experimental.pallas import tpu as pltpu

F32 = jnp.float32
BF16 = jnp.bfloat16
I32 = jnp.int32

EPS = 1e-6
HEAD_DIM = 128
Q_PER_KV = 3
GRID_W = 64
ROPE_THETA = 10000.0
POOL_WINDOWS = (2, 4, 8, 16)
TOP_K = 4
SWIGLU_ALPHA = 1.702
SWIGLU_LIMIT = 7.0
N_MOD = 6
ATTN_SCALE = HEAD_DIM ** -0.5
LOG2_E = 1.4426950408889634

V7X_VMEM_BYTES = 64 * 1024 * 1024
VMEM_LIMIT = V7X_VMEM_BYTES * 7 // 8
LANES = 128
SUBLANES = 8
ATTN_ROW_BLOCK = 128
NORM_ROW_CHUNK = 256
MOD_ROWS = 16
HI_MASK = -65536


def _params(*sem):
    return pltpu.CompilerParams(dimension_semantics=sem, vmem_limit_bytes=VMEM_LIMIT)


def _pack_bf16_pair(lo, hi):
    lo_bits = lax.bitcast_convert_type(lo.astype(BF16).astype(F32), I32)
    hi_bits = lax.bitcast_convert_type(hi.astype(BF16).astype(F32), I32)
    return lax.shift_right_logical(lo_bits, 16) | hi_bits


def _unpack_bf16_pair(p):
    lo = lax.bitcast_convert_type(lax.shift_left(p, 16), F32)
    hi = lax.bitcast_convert_type(p & HI_MASK, F32)
    return lo, hi


def _adaln_kernel(c_ref, w_ref, b_ref, o_ref):
    c = c_ref[...]
    a = (c * jax.nn.sigmoid(c)).astype(BF16)
    o_ref[...] = jnp.dot(a, w_ref[...].astype(BF16), preferred_element_type=F32) + b_ref[...]


def _adaln(cc, w_ada, b_ada):
    d = cc.shape[1]
    n = w_ada.shape[2]
    tn = min(512, n)
    return pl.pallas_call(
        _adaln_kernel,
        grid=(n // tn,),
        in_specs=[
            pl.BlockSpec((MOD_ROWS, d), lambda j: (0, 0)),
            pl.BlockSpec((None, d, tn), lambda j: (0, 0, j)),
            pl.BlockSpec((1, tn), lambda j: (0, j)),
        ],
        out_specs=pl.BlockSpec((MOD_ROWS, tn), lambda j: (0, j)),
        out_shape=jax.ShapeDtypeStruct((MOD_ROWS, n), F32),
        compiler_params=_params("arbitrary"),
    )(cc, w_ada, b_ada)


def _inproj_kernel(x_ref, mod_ref, g_ref, w_ref, qg_ref, kg_ref, cos_ref, sin_ref, o_ref, h_scr, z_scr,
                   *, nj, nq, nk, rope, heads_per_tile):
    j = pl.program_id(2)

    @pl.when(j == 0)
    def _():
        rc = min(NORM_ROW_CHUNK, x_ref.shape[0])

        def chunk(r, carry):
            rows = pl.ds(pl.multiple_of(r * rc, rc), rc)
            xf = x_ref[rows, :]
            ms = jnp.mean(xf * xf, axis=-1, keepdims=True)
            y = xf * lax.rsqrt(ms + EPS) * g_ref[...]
            h_scr[rows, :] = (y * (1 + mod_ref[1:2, :]) + mod_ref[0:1, :]).astype(BF16)
            return carry

        lax.fori_loop(0, x_ref.shape[0] // rc, chunk, 0)

        z_scr[...] = jnp.zeros_like(z_scr)

    def finish_previous_tile(z_prev):
        t = j - 1
        plain = t >= nq + nk
        g = jnp.where(t < nq, qg_ref[...], kg_ref[...])
        for hh in range(heads_per_tile):
            zc = z_prev[:, hh * HEAD_DIM:(hh + 1) * HEAD_DIM]
            ms = jnp.mean(zc * zc, axis=-1, keepdims=True)
            zn = zc * lax.rsqrt(ms + EPS) * g
            if rope:
                zn = zn * cos_ref[...] + pltpu.roll(zn, HEAD_DIM // 2, 1) * sin_ref[...]
            o_ref[:, hh * HEAD_DIM:(hh + 1) * HEAD_DIM] = jnp.where(plain, zc, zn).astype(BF16)

    for par in range(2):
        @pl.when((j < nj) & (j % 2 == par))
        def _(par=par):
            finish_previous_tile(z_scr.at[1 - par])
            z_scr[par] = jnp.dot(h_scr[...], w_ref[...], preferred_element_type=F32)

    @pl.when(j == nj)
    def _():
        finish_previous_tile(z_scr.at[(nj - 1) % 2])


def _inproj(x, mod3, mod_row, norm_g, w_in, col0, ncols, nq_cols, nk_cols, q_g, k_g, cosf, sinf, rope):
    b, l, d = x.shape
    tm = min(1024, l)
    tn = min(512, ncols)
    nj = ncols // tn
    kern = functools.partial(_inproj_kernel, nj=nj, nq=nq_cols // tn, nk=nk_cols // tn, rope=rope,
                             heads_per_tile=tn // HEAD_DIM)
    if mod_row is None:
        mod_map = lambda bi, i, j: (bi, 0, 0)
    else:
        mod_map = lambda bi, i, j: (mod_row, 0, 0)
    return pl.pallas_call(
        kern,
        grid=(b, l // tm, nj + 1),
        in_specs=[
            pl.BlockSpec((None, tm, d), lambda bi, i, j: (bi, i, 0), pipeline_mode=pl.Buffered(1)),
            pl.BlockSpec((None, N_MOD, d), mod_map),
            pl.BlockSpec((1, d), lambda bi, i, j: (0, 0)),
            pl.BlockSpec((None, d, tn), lambda bi, i, j: (0, 0, jnp.minimum(j, nj - 1) + col0 // tn)),
            pl.BlockSpec((1, HEAD_DIM), lambda bi, i, j: (0, 0)),
            pl.BlockSpec((1, HEAD_DIM), lambda bi, i, j: (0, 0)),
            pl.BlockSpec((tm, HEAD_DIM), lambda bi, i, j: (i, 0)),
            pl.BlockSpec((tm, HEAD_DIM), lambda bi, i, j: (i, 0)),
        ],
        out_specs=pl.BlockSpec((None, tm, tn), lambda bi, i, j: (bi, i, jnp.maximum(j - 1, 0))),
        out_shape=jax.ShapeDtypeStruct((b, l, ncols), BF16),
        scratch_shapes=[pltpu.VMEM((tm, d), BF16), pltpu.VMEM((2, tm, tn), F32)],
        compiler_params=_params("arbitrary", "arbitrary", "arbitrary"),
    )(x, mod3, norm_g, w_in, q_g, k_g, cosf, sinf)


def _attn_kernel(q_ref, kc_ref, vc_ref, k_ref, v_ref, o_ref):
    nt = (((1,), (1,)), ((), ()))
    vc = jnp.concatenate([vc_ref[...], jnp.ones(vc_ref.shape, BF16)], axis=1)
    vl = jnp.concatenate([v_ref[...], jnp.ones(v_ref.shape, BF16)], axis=1)
    tq = q_ref.shape[0]
    rb = min(ATTN_ROW_BLOCK, tq)
    blocks = [(r0, g * HEAD_DIM) for g in range(Q_PER_KV) for r0 in range(0, tq, rb)]

    def scores(blk):
        r0, c0 = blk
        q = q_ref[r0:r0 + rb, c0:c0 + HEAD_DIM]
        return (lax.dot_general(q, kc_ref[...], nt, preferred_element_type=F32),
                lax.dot_general(q, k_ref[...], nt, preferred_element_type=F32))

    nxt = scores(blocks[0])
    for bi, (r0, c0) in enumerate(blocks):
        sc, sl = nxt
        if bi + 1 < len(blocks):
            nxt = scores(blocks[bi + 1])
        m = jnp.maximum(jnp.max(sc, axis=-1, keepdims=True), jnp.max(sl, axis=-1, keepdims=True))
        pc = jnp.exp2((sc - m).astype(BF16))
        pk = jnp.exp2((sl - m).astype(BF16))
        o = jnp.dot(pc, vc, preferred_element_type=F32) + jnp.dot(pk, vl, preferred_element_type=F32)
        o_ref[r0:r0 + rb, c0:c0 + HEAD_DIM] = (o[:, :HEAD_DIM] / o[:, HEAD_DIM:HEAD_DIM + 1]).astype(BF16)


def _attention(z, zc, n_kv):
    b, s, _ = z.shape
    c = zc.shape[1]
    tq = min(512, s)
    gw = Q_PER_KV * HEAD_DIM
    n_q = n_kv * Q_PER_KV
    return pl.pallas_call(
        _attn_kernel,
        grid=(b, n_kv, s // tq),
        in_specs=[
            pl.BlockSpec((None, tq, gw), lambda bi, g, i: (bi, i, g)),
            pl.BlockSpec((None, c, HEAD_DIM), lambda bi, g, i: (bi, 0, g)),
            pl.BlockSpec((None, c, HEAD_DIM), lambda bi, g, i: (bi, 0, n_kv + g)),
            pl.BlockSpec((None, s, HEAD_DIM), lambda bi, g, i: (bi, 0, n_q + g)),
            pl.BlockSpec((None, s, HEAD_DIM), lambda bi, g, i: (bi, 0, n_q + n_kv + g)),
        ],
        out_specs=pl.BlockSpec((None, tq, gw), lambda bi, g, i: (bi, i, g)),
        out_shape=jax.ShapeDtypeStruct((b, s, n_q * HEAD_DIM), BF16),
        compiler_params=_params("arbitrary", "arbitrary", "arbitrary"),
    )(z, zc, zc, z, z)


def _pool_kernel(u_ref, w_ref, sc_ref, o_ref, d_scr):
    g = pl.program_id(1)
    n, gw = u_ref.shape
    t = lax.broadcasted_iota(I32, (n, LANES), 0)

    def shifted(a, d):
        r = pltpu.roll(a, (-d) % n, 0)
        ok = (t + d < n) if d > 0 else (t + d >= 0)
        return jnp.where(ok, r, 0.0)

    for gi, w in enumerate(POOL_WINDOWS):
        @pl.when(g == gi)
        def _(w=w):
            half = w // 2
            cnt = (jnp.minimum(t + half, n) - jnp.maximum(t - half, 0)).astype(F32)
            for c0 in range(0, gw, LANES):
                uf = u_ref[:, c0:c0 + LANES].astype(F32)
                fwd = uf
                bwd = shifted(uf, -1)
                span = 1
                while span < half:
                    fwd = fwd + shifted(fwd, span)
                    bwd = bwd + shifted(bwd, -span)
                    span *= 2
                d_scr[:, c0:c0 + LANES] = ((fwd + bwd) / cnt - uf).astype(BF16)

    y = jnp.dot(d_scr[...], w_ref[...].astype(BF16), preferred_element_type=F32)
    o_ref[...] = (y * sc_ref[...]).astype(BF16)


def _pool(z, w_pool, pool_scale, col0):
    b, s, _ = z.shape
    ng, gw = w_pool.shape[1], w_pool.shape[2]
    return pl.pallas_call(
        _pool_kernel,
        grid=(b, ng),
        in_specs=[
            pl.BlockSpec((None, s, gw), lambda bi, g: (bi, 0, col0 // gw + g)),
            pl.BlockSpec((None, None, gw, gw), lambda bi, g: (0, g, 0, 0)),
            pl.BlockSpec((None, 1, gw), lambda bi, g: (g, 0, 0)),
        ],
        out_specs=pl.BlockSpec((None, s, gw), lambda bi, g: (bi, 0, g)),
        out_shape=jax.ShapeDtypeStruct((b, s, ng * gw), BF16),
        scratch_shapes=[pltpu.VMEM((s, gw), BF16)],
        compiler_params=_params("arbitrary", "arbitrary"),
    )(z, w_pool, pool_scale.reshape(ng, 1, gw))


def _outproj_kernel(a_ref, p_ref, wa_ref, wp_ref, x_ref, mod_ref, o_ref):
    y = jnp.dot(a_ref[...], wa_ref[...], preferred_element_type=F32)
    y = y + jnp.dot(p_ref[...], wp_ref[...], preferred_element_type=F32)
    o_ref[...] = x_ref[...] + mod_ref[2:3, :] * y


def _outproj(attn, pool, w_out, x, mod3):
    b, s, d = x.shape
    aw, pw = attn.shape[2], pool.shape[2]
    tm = min(1024, s)
    tn = min(512, d)
    return pl.pallas_call(
        _outproj_kernel,
        grid=(b, s // tm, d // tn),
        in_specs=[
            pl.BlockSpec((None, tm, aw), lambda bi, i, j: (bi, i, 0)),
            pl.BlockSpec((None, tm, pw), lambda bi, i, j: (bi, i, 0)),
            pl.BlockSpec((None, aw, tn), lambda bi, i, j: (0, 0, j)),
            pl.BlockSpec((None, pw, tn), lambda bi, i, j: (0, aw // pw, j)),
            pl.BlockSpec((None, tm, tn), lambda bi, i, j: (bi, i, j)),
            pl.BlockSpec((None, N_MOD, tn), lambda bi, i, j: (bi, 0, j)),
        ],
        out_specs=pl.BlockSpec((None, tm, tn), lambda bi, i, j: (bi, i, j)),
        out_shape=jax.ShapeDtypeStruct((b, s, d), F32),
        compiler_params=_params("arbitrary", "arbitrary", "arbitrary"),
    )(attn, pool, w_out, w_out, x, mod3)


def _router_kernel(x_ref, mod_ref, g_ref, wr_ref, br_ref, hp_ref, mi_ref, mg_ref, cnt_ref, carry_scr,
                   *, tm, ne):
    first = (pl.program_id(0) == 0) & (pl.program_id(1) == 0)

    @pl.when(first)
    def _():
        carry_scr[...] = jnp.zeros_like(carry_scr)

    xf = x_ref[...]
    ms = jnp.mean(xf * xf, axis=-1, keepdims=True)
    y = xf * lax.rsqrt(ms + EPS) * g_ref[...]
    h = y * (1 + mod_ref[4:5, :]) + mod_ref[3:4, :]
    half = h.shape[1] // 2
    hp_ref[...] = _pack_bf16_pair(h[:, :half], h[:, half:])
    logits = jnp.dot(h.astype(BF16), wr_ref[...].astype(BF16), preferred_element_type=F32) + br_ref[...]

    lane = lax.broadcasted_iota(I32, logits.shape, 1).astype(F32)
    work = logits
    chosen = jnp.zeros(logits.shape, jnp.bool_)
    hits, idxs, vals = [], [], []
    for _ in range(TOP_K):
        m = jnp.max(work, axis=-1, keepdims=True)
        ik = jnp.min(jnp.where(work == m, lane, float(ne)), axis=-1, keepdims=True)
        hit = lane == ik
        hits.append(hit)
        idxs.append(ik.astype(I32))
        vals.append(m)
        chosen = chosen | hit
        work = jnp.where(hit, -jnp.inf, work)

    exps = [jnp.exp(v - vals[0]) for v in vals]
    den = exps[0]
    for e in exps[1:]:
        den = den + e
    gates = [e / den for e in exps]

    sel = chosen.astype(F32)
    r = lax.broadcasted_iota(I32, (tm, tm), 0)
    c = lax.broadcasted_iota(I32, (tm, tm), 1)
    tri = (c < r).astype(BF16)
    rank = carry_scr[...] + jnp.dot(tri, sel.astype(BF16), preferred_element_type=F32)
    carry_scr[...] = carry_scr[...] + jnp.sum(sel, axis=0, keepdims=True)
    cnt_ref[...] = carry_scr[...]
    ranks = [jnp.sum(jnp.where(hit, rank, 0.0), axis=-1, keepdims=True).astype(I32) for hit in hits]

    lane128 = lax.broadcasted_iota(I32, (tm, LANES), 1)
    mi = jnp.zeros((tm, LANES), I32)
    mg = jnp.zeros((tm, LANES), F32)
    for k in range(TOP_K):
        mi = jnp.where(lane128 == k, idxs[k], mi)
        mi = jnp.where(lane128 == TOP_K + k, ranks[k], mi)
        mg = jnp.where(lane128 == k, gates[k], mg)
    mi_ref[...] = mi
    mg_ref[...] = mg


def _router(x1, mod3, norm_g, w_router, b_router):
    b, s, d = x1.shape
    ne = w_router.shape[2]
    tm = min(256, s)
    nt = s // tm
    n = b * s
    kern = functools.partial(_router_kernel, tm=tm, ne=ne)
    return pl.pallas_call(
        kern,
        grid=(b, nt),
        in_specs=[
            pl.BlockSpec((None, tm, d), lambda bi, i: (bi, i, 0)),
            pl.BlockSpec((None, N_MOD, d), lambda bi, i: (bi, 0, 0)),
            pl.BlockSpec((1, d), lambda bi, i: (0, 0)),
            pl.BlockSpec((None, d, ne), lambda bi, i: (0, 0, 0)),
            pl.BlockSpec((1, ne), lambda bi, i: (0, 0)),
        ],
        out_specs=[
            pl.BlockSpec((tm, d // 2), lambda bi, i: (bi * nt + i, 0)),
            pl.BlockSpec((tm, LANES), lambda bi, i: (bi * nt + i, 0)),
            pl.BlockSpec((tm, LANES), lambda bi, i: (bi * nt + i, 0)),
            pl.BlockSpec((1, ne), lambda bi, i: (0, 0)),
        ],
        out_shape=[
            jax.ShapeDtypeStruct((n, d // 2), I32),
            jax.ShapeDtypeStruct((n, LANES), I32),
            jax.ShapeDtypeStruct((n, LANES), F32),
            jax.ShapeDtypeStruct((1, ne), F32),
        ],
        scratch_shapes=[pltpu.VMEM((1, ne), F32)],
        compiler_params=_params("arbitrary", "arbitrary"),
    )(x1, mod3, norm_g, w_router, b_router)


DISPATCH_WINDOW = 128
DMA_PRIORITIES = 2
MOE_TILE_ROWS = 1024
MOE_SUB_ROWS = 256
MOE_K_CHUNK = 512


def _dispatch_kernel(pos_ref, zstart_ref, zlen_ref, hp_ref, xs_ref, zero_scr, sem, zsem, *, td, ne):
    bits = [1 << k for k in reversed(range(SUBLANES.bit_length() - 1, MOE_SUB_ROWS.bit_length() - 1))]

    def row_copy(e, r):
        return pltpu.make_async_copy(zero_scr.at[pl.ds(0, 1)], xs_ref.at[pl.ds(zstart_ref[e] + r, 1)], zsem)

    def chunk_copy(e, head, bit):
        off = pl.multiple_of(zstart_ref[e] + head + ((zlen_ref[e] - head) & ~(2 * bit - 1)), SUBLANES)
        return pltpu.make_async_copy(zero_scr.at[pl.ds(0, bit)], xs_ref.at[pl.ds(off, bit)], zsem)

    def zero_pass(start):
        def per_expert(e, carry):
            head = (-zstart_ref[e]) & (SUBLANES - 1)
            for r in range(SUBLANES - 1):
                @pl.when(r < head)
                def _(r=r):
                    row_copy(e, r).start() if start else row_copy(e, r).wait()
            for bit in bits:
                @pl.when(((zlen_ref[e] - head) & bit) != 0)
                def _(bit=bit):
                    chunk_copy(e, head, bit).start() if start else chunk_copy(e, head, bit).wait()
            return carry
        lax.fori_loop(0, ne, per_expert, 0)

    @pl.when(pl.program_id(0) == 0)
    def _():
        zero_scr[...] = jnp.zeros_like(zero_scr)
        zero_pass(True)
        zero_pass(False)

    base = pl.program_id(0) * (td * TOP_K)

    def copy(t, k):
        p = pos_ref[base + t * TOP_K + k]
        return pltpu.make_async_copy(hp_ref.at[pl.ds(t, 1)], xs_ref.at[pl.ds(p, 1)], sem)

    def body(t, carry):
        for k in range(TOP_K):
            copy(t, k).start(priority=k % DMA_PRIORITIES)

        @pl.when(t >= DISPATCH_WINDOW)
        def _():
            for k in range(TOP_K):
                copy(t - DISPATCH_WINDOW, k).wait()

        return carry

    lax.fori_loop(0, td, body, 0)

    def drain(t, carry):
        for k in range(TOP_K):
            copy(t, k).wait()
        return carry

    lax.fori_loop(td - DISPATCH_WINDOW, td, drain, 0)


def _dispatch(pos_flat, zstart, zlen, hp, rows):
    n, w = hp.shape
    td = min(512, n)
    kern = functools.partial(_dispatch_kernel, td=td, ne=zstart.shape[0])
    return pl.pallas_call(
        kern,
        grid_spec=pltpu.PrefetchScalarGridSpec(
            num_scalar_prefetch=3,
            grid=(n // td,),
            in_specs=[pl.BlockSpec((td, w), lambda i, pos, zs, zl: (i, 0))],
            out_specs=pl.BlockSpec(memory_space=pl.ANY),
            scratch_shapes=[pltpu.VMEM((MOE_SUB_ROWS // 2, w), I32), pltpu.SemaphoreType.DMA,
                            pltpu.SemaphoreType.DMA],
        ),
        out_shape=jax.ShapeDtypeStruct((rows, w), I32),
        compiler_params=_params("arbitrary"),
    )(pos_flat, zstart, zlen, hp)


def _chunked_dots(x_ref, m, w_refs, kc):
    d = w_refs[0].shape[0]
    cast = lambda k0: [w[k0:k0 + kc, :].astype(BF16) for w in w_refs]
    accs = [None] * len(w_refs)
    nxt = cast(0)
    for k0 in range(0, d, kc):
        cur = nxt
        if k0 + kc < d:
            nxt = cast(k0 + kc)
        xk = x_ref[0:m, k0:k0 + kc]
        for n, wk in enumerate(cur):
            part = jnp.dot(xk, wk, preferred_element_type=F32)
            accs[n] = part if accs[n] is None else accs[n] + part
    return accs


def _moe_up_kernel(te_ref, tr_ref, nu_ref, xs_ref, wg_ref, bg_ref, wu_ref, bu_ref, hid_ref, xb_scr, *, sub):
    i = pl.program_id(0)
    j = pl.program_id(1)
    rows = tr_ref[i]
    tm, half = xs_ref.shape
    n_sub = (rows + sub - 1) // sub

    for v in range(1, tm // sub + 1):
        m = v * sub

        @pl.when(n_sub == v)
        def _(m=m):
            @pl.when(j == 0)
            def _():
                lo, hi = _unpack_bf16_pair(xs_ref[0:m, :])
                xb_scr[0:m, :half] = lo.astype(BF16)
                xb_scr[0:m, half:] = hi.astype(BF16)

            a, u = _chunked_dots(xb_scr, m, (wg_ref, wu_ref), MOE_K_CHUNK)
            a = jnp.minimum(a + bg_ref[...], SWIGLU_LIMIT)
            u = jnp.clip(u + bu_ref[...], -SWIGLU_LIMIT, SWIGLU_LIMIT)
            hid_ref[0:m, :] = ((u + 1) * (a * jax.nn.sigmoid(SWIGLU_ALPHA * a))).astype(BF16)
            if m < tm:
                hid_ref[m:tm, :] = jnp.zeros((tm - m, hid_ref.shape[1]), BF16)

    @pl.when(n_sub == 0)
    def _():
        hid_ref[...] = jnp.zeros_like(hid_ref)


def _moe_index_helpers(n_col_blocks):
    def row(i, nu):
        return jnp.minimum(i, nu[0] - 1)

    def col(i, j, nu):
        return jnp.where(i < nu[0], j, n_col_blocks - 1)

    return row, col


def _moe_up(tile_expert, tile_rows, n_used, xs, w_gate, b_gate, w_up, b_up, tm, sub):
    rows, half = xs.shape
    d = 2 * half
    ne, ff = w_gate.shape[1], w_gate.shape[3]
    tf = min(256, ff)
    nf = ff // tf
    row, col = _moe_index_helpers(nf)
    w_spec = pl.BlockSpec((None, None, d, tf), lambda i, j, te, tr, nu: (0, te[row(i, nu)], 0, col(i, j, nu)))
    b_spec = pl.BlockSpec((None, 1, tf), lambda i, j, te, tr, nu: (te[row(i, nu)], 0, col(i, j, nu)))
    return pl.pallas_call(
        functools.partial(_moe_up_kernel, sub=sub),
        grid_spec=pltpu.PrefetchScalarGridSpec(
            num_scalar_prefetch=3,
            grid=(rows // tm, nf),
            in_specs=[
                pl.BlockSpec((tm, half), lambda i, j, te, tr, nu: (row(i, nu), 0)),
                w_spec, b_spec, w_spec, b_spec,
            ],
            out_specs=pl.BlockSpec((tm, tf), lambda i, j, te, tr, nu: (i, j)),
            scratch_shapes=[pltpu.VMEM((tm, d), BF16)],
        ),
        out_shape=jax.ShapeDtypeStruct((rows, ff), BF16),
        compiler_params=_params("arbitrary", "arbitrary"),
    )(tile_expert, tile_rows, n_used, xs, w_gate, b_gate.reshape(ne, 1, ff), w_up, b_up.reshape(ne, 1, ff))


def _moe_down_kernel(te_ref, tr_ref, nu_ref, hid_ref, wlo_ref, whi_ref, blo_ref, bhi_ref, ys_ref, *, sub):
    rows = tr_ref[pl.program_id(0)]
    tm = hid_ref.shape[0]
    n_sub = (rows + sub - 1) // sub

    for v in range(1, tm // sub + 1):
        m = v * sub

        @pl.when(n_sub == v)
        def _(m=m):
            ylo, yhi = _chunked_dots(hid_ref, m, (wlo_ref, whi_ref), MOE_K_CHUNK)
            ys_ref[0:m, :] = _pack_bf16_pair(ylo + blo_ref[...], yhi + bhi_ref[...])
            if m < tm:
                ys_ref[m:tm, :] = jnp.zeros((tm - m, ys_ref.shape[1]), I32)

    @pl.when(n_sub == 0)
    def _():
        ys_ref[...] = jnp.zeros_like(ys_ref)


def _moe_down(tile_expert, tile_rows, n_used, hid, w_down, b_down, tm, sub):
    rows, ff = hid.shape
    ne, d = w_down.shape[1], w_down.shape[3]
    half = d // 2
    tn = min(512, half)
    nj = half // tn
    row, col = _moe_index_helpers(nj)

    def w_spec(off):
        return pl.BlockSpec((None, None, ff, tn),
                            lambda i, j, te, tr, nu: (0, te[row(i, nu)], 0, col(i, j, nu) + off))

    def b_spec(off):
        return pl.BlockSpec((None, 1, tn), lambda i, j, te, tr, nu: (te[row(i, nu)], 0, col(i, j, nu) + off))

    return pl.pallas_call(
        functools.partial(_moe_down_kernel, sub=sub),
        grid_spec=pltpu.PrefetchScalarGridSpec(
            num_scalar_prefetch=3,
            grid=(rows // tm, nj),
            in_specs=[
                pl.BlockSpec((tm, ff), lambda i, j, te, tr, nu: (row(i, nu), 0)),
                w_spec(0), w_spec(nj), b_spec(0), b_spec(nj),
            ],
            out_specs=pl.BlockSpec((tm, tn), lambda i, j, te, tr, nu: (i, j)),
        ),
        out_shape=jax.ShapeDtypeStruct((rows, half), I32),
        compiler_params=_params("arbitrary", "arbitrary"),
    )(tile_expert, tile_rows, n_used, hid, w_down, w_down, b_down.reshape(ne, 1, d), b_down.reshape(ne, 1, d))


def _combine_kernel(pos_ref, ys_ref, x_ref, mg_ref, mod_ref, fg_ref, o_ref, ybuf, sem, *, tc, nt):
    base = (pl.program_id(0) * nt + pl.program_id(1)) * (tc * TOP_K)

    def copy(t, k):
        p = pos_ref[base + t * TOP_K + k]
        return pltpu.make_async_copy(ys_ref.at[pl.ds(p, 1)], ybuf.at[k, pl.ds(t, 1)], sem)

    def issue(t, carry):
        for k in range(TOP_K):
            copy(t, k).start(priority=k % DMA_PRIORITIES)
        return carry

    def wait(t, carry):
        for k in range(TOP_K):
            copy(t, k).wait()
        return carry

    lax.fori_loop(0, tc, issue, 0)
    lax.fori_loop(0, tc, wait, 0)

    half = ybuf.shape[2]
    acc_lo = jnp.zeros((tc, half), F32)
    acc_hi = jnp.zeros((tc, half), F32)
    for k in range(TOP_K):
        lo, hi = _unpack_bf16_pair(ybuf[k])
        g = mg_ref[:, k:k + 1]
        acc_lo = acc_lo + lo * g
        acc_hi = acc_hi + hi * g
    x_lo = x_ref[:, :half] + mod_ref[5:6, :half] * acc_lo
    x_hi = x_ref[:, half:] + mod_ref[5:6, half:] * acc_hi
    ms = (jnp.sum(x_lo * x_lo, axis=-1, keepdims=True) + jnp.sum(x_hi * x_hi, axis=-1, keepdims=True)) / (2 * half)
    inv = lax.rsqrt(ms + EPS)
    o_ref[:, :half] = x_lo * inv * fg_ref[:, :half]
    o_ref[:, half:] = x_hi * inv * fg_ref[:, half:]


def _combine(pos_flat, ys, x1, mg, mod3, final_g):
    b, s, d = x1.shape
    tc = min(256, s)
    nt = s // tc
    kern = functools.partial(_combine_kernel, tc=tc, nt=nt)
    return pl.pallas_call(
        kern,
        grid_spec=pltpu.PrefetchScalarGridSpec(
            num_scalar_prefetch=1,
            grid=(b, nt),
            in_specs=[
                pl.BlockSpec(memory_space=pl.ANY),
                pl.BlockSpec((None, tc, d), lambda bi, i, pos: (bi, i, 0)),
                pl.BlockSpec((tc, LANES), lambda bi, i, pos: (bi * nt + i, 0)),
                pl.BlockSpec((None, N_MOD, d), lambda bi, i, pos: (bi, 0, 0)),
                pl.BlockSpec((1, d), lambda bi, i, pos: (0, 0)),
            ],
            out_specs=pl.BlockSpec((None, tc, d), lambda bi, i, pos: (bi, i, 0)),
            scratch_shapes=[pltpu.VMEM((TOP_K, tc, d // 2), I32), pltpu.SemaphoreType.DMA],
        ),
        out_shape=jax.ShapeDtypeStruct((b, s, d), F32),
        compiler_params=_params("arbitrary", "arbitrary"),
    )(pos_flat, ys, x1, mg, mod3, final_g)


def _rope_tables(seq_len):
    pairs = HEAD_DIM // 4
    rows = seq_len // GRID_W
    row = jnp.repeat(jnp.arange(rows, dtype=F32), GRID_W)
    col = jnp.tile(jnp.arange(GRID_W, dtype=F32), rows)
    freqs = ROPE_THETA ** (-jnp.arange(pairs, dtype=F32) / pairs)
    ang = jnp.concatenate([row[:, None] * freqs, col[:, None] * freqs], axis=-1)
    cos, sin = jnp.cos(ang), jnp.sin(ang)
    return jnp.concatenate([cos, cos], axis=-1), jnp.concatenate([-sin, sin], axis=-1)


def _routing_tables(counts, n_assign, tm, sub):
    ne = counts.shape[0]
    nt = (counts + tm - 1) // tm
    rpt = jnp.maximum(((counts + jnp.maximum(nt, 1) - 1) // jnp.maximum(nt, 1) + sub - 1) // sub * sub, sub)
    pend = jnp.cumsum(nt * tm)
    pstart = pend - nt * tm
    n_tiles = n_assign // tm + ne
    tile_row0 = jnp.arange(n_tiles, dtype=I32) * tm
    tile_expert = jnp.minimum(jnp.sum((pend[None, :] <= tile_row0[:, None]).astype(I32), axis=1), ne - 1)
    tile_k = (tile_row0 - pstart[tile_expert]) // tm
    tile_rows = jnp.clip(counts[tile_expert] - tile_k * rpt[tile_expert], 0, rpt[tile_expert])
    tile_rows = jnp.where(tile_row0 < pend[-1], tile_rows, 0).astype(I32)
    n_used = (pend[-1:] // tm).astype(I32)
    last_rows = counts - (nt - 1) * rpt
    zstart = jnp.where(counts > 0, pstart + (nt - 1) * tm + last_rows, pstart).astype(I32)
    zlen = jnp.where(counts > 0, (last_rows + sub - 1) // sub * sub - last_rows, 0).astype(I32)
    return pstart, rpt, n_tiles, tile_expert.astype(I32), tile_rows, n_used, zstart, zlen


def kernel(x, c, ctx, c_ctx, w_ada, b_ada, norm1_g, w_in, q_norm_g, k_norm_g, w_pool, pool_scale, w_out,
           norm2_g, w_router, b_router, w_gate, b_gate, w_up, b_up, w_down, b_down, final_g):
    b, s, d = x.shape
    assert w_ada.shape[0] == 1, "single-layer trunk"
    assert b + 1 <= MOD_ROWS
    pool_w = d // 4
    attn_w = d - pool_w
    n_kv = attn_w // HEAD_DIM // Q_PER_KV
    kv_w = n_kv * HEAD_DIM
    ne = w_router.shape[2]

    cc = jnp.concatenate([c, c_ctx[None, :], jnp.zeros((MOD_ROWS - b - 1, d), F32)], axis=0)
    mod3 = _adaln(cc, w_ada, b_ada).reshape(MOD_ROWS, N_MOD, d)

    cosf, sinf = _rope_tables(s)
    w_in = w_in.astype(BF16)
    w_out = w_out.astype(BF16)
    q_gain = q_norm_g * (ATTN_SCALE * LOG2_E)
    z = _inproj(x, mod3, None, norm1_g, w_in, 0, attn_w + 2 * kv_w + pool_w, attn_w, kv_w,
                q_gain, k_norm_g, cosf, sinf, True)
    zc = _inproj(ctx, mod3, b, norm1_g, w_in, attn_w, 2 * kv_w, 0, kv_w,
                 q_norm_g, k_norm_g, cosf, sinf, False)
    attn = _attention(z, zc, n_kv)
    pool = _pool(z, w_pool, pool_scale, attn_w + 2 * kv_w)
    x1 = _outproj(attn, pool, w_out, x, mod3)

    hp, mi, mg, counts = _router(x1, mod3, norm2_g, w_router, b_router)
    n = b * s
    tm = min(MOE_TILE_ROWS, n)
    sub = min(MOE_SUB_ROWS, tm)
    pstart, rpt, n_tiles, tile_expert, tile_rows, n_used, zstart, zlen = _routing_tables(
        counts[0].astype(I32), n * TOP_K, tm, sub)
    idx, rank = mi[:, :TOP_K], mi[:, TOP_K:2 * TOP_K]
    tile_k = jnp.floor((rank.astype(F32) + 0.5) / rpt[idx].astype(F32)).astype(I32)
    pos_flat = (pstart[idx] + tile_k * tm + (rank - tile_k * rpt[idx])).reshape(n * TOP_K)

    xs = _dispatch(pos_flat, zstart, zlen, hp, n_tiles * tm)
    hid = _moe_up(tile_expert, tile_rows, n_used, xs, w_gate, b_gate[0], w_up, b_up[0], tm, sub)
    ys = _moe_down(tile_expert, tile_rows, n_used, hid, w_down, b_down[0], tm, sub)
    return _combine(pos_flat, ys, x1, mg, mod3, final_g[None, :])
```
